```python
import math
import jax, jax.numpy as jnp
from jax import lax
import numpy as np

D_MODEL = 1024
BATCH = 8
SEQ = 8192
DEPTH = 1

HEAD_DIM = 64
N_ATTN_HEADS = 8
N_DELTA_HEADS = 8
ATTN_WIDTH = N_ATTN_HEADS * HEAD_DIM
DELTA_WIDTH = N_DELTA_HEADS * HEAD_DIM
MIX_WIDTH = ATTN_WIDTH + DELTA_WIDTH
DILATED_BRANCHES = ((128, 1), (512, 4), (2048, 16))
PAD_UNIT = 2048
N_BUCKETS = 32
MAX_DISTANCE = 2048
CONV_WIDTH = 4
CHUNK = 64
D_FF = (8 * D_MODEL + 3 * 256 - 1) // (3 * 256) * 256
IN_WIDTH = 3 * ATTN_WIDTH + 4 * DELTA_WIDTH + 2 * N_DELTA_HEADS
EPS = 1e-6
NEG_INF = -1e30

kernel_name = 'hybrid_dilated_attn_gated_deltanet_block'


def _rmsnorm(x, g):
    x32 = x.astype(jnp.float32)
    y = x32 * lax.rsqrt(jnp.mean(x32 * x32, axis=-1, keepdims=True) + EPS)
    return (y * g.astype(jnp.float32)).astype(x.dtype)


def _l2norm(x):
    return x * lax.rsqrt(jnp.sum(x * x, axis=-1, keepdims=True) + EPS)


def _t5_bucket(distance):
    max_exact = N_BUCKETS // 2
    dist_f = jnp.maximum(distance, 1).astype(jnp.float32)
    large = max_exact + (jnp.log(dist_f / max_exact) / math.log(MAX_DISTANCE / max_exact)
                         * (N_BUCKETS - max_exact)).astype(jnp.int32)
    return jnp.where(distance < max_exact, distance, jnp.minimum(large, N_BUCKETS - 1))


def _dilated_branch(q, k, v, rel_bias, window, dilation):
    b, h, p, dh = q.shape
    band = window // dilation
    length = p // dilation
    n_blocks = length // band

    def to_blocks(t):
        t = t.reshape(b, h, length, dilation, dh).transpose(0, 1, 3, 2, 4)
        return t.reshape(b, h, dilation, n_blocks, band, dh)

    qb, kb, vb = to_blocks(q), to_blocks(k), to_blocks(v)

    def with_prev(t):
        prev = jnp.pad(t, ((0, 0), (0, 0), (0, 0), (1, 0), (0, 0), (0, 0)))[:, :, :, :-1]
        return jnp.concatenate([prev, t], axis=-2)

    kw, vw = with_prev(kb), with_prev(vb)
    qi = jnp.arange(band)[:, None]
    kj = jnp.arange(2 * band)[None, :]
    steps = qi + band - kj
    in_window = (steps >= 0) & (steps <= band)
    not_before_start = (jnp.arange(n_blocks)[:, None, None] > 0) | (kj >= band)[None]
    valid = in_window[None] & not_before_start
    bias = rel_bias.astype(jnp.float32)[_t5_bucket(jnp.maximum(steps, 0) * dilation)]
    bias = bias.transpose(2, 0, 1)
    s = jnp.einsum('bhrnqd,bhrnkd->bhrnqk', qb * dh ** -0.5, kw) + bias[None, :, None, None]
    s = jnp.where(valid, s, NEG_INF)
    m = jnp.max(s, axis=-1, keepdims=True)
    e = jnp.exp(s - m)
    denom = jnp.sum(e, axis=-1, keepdims=True)
    o = jnp.einsum('bhrnqk,bhrnkd->bhrnqd', e, vw) / denom
    lse = (m + jnp.log(denom))[..., 0]
    o = o.reshape(b, h, dilation, length, dh).transpose(0, 1, 3, 2, 4).reshape(b, h, p, dh)
    lse = lse.reshape(b, h, dilation, length).transpose(0, 1, 3, 2).reshape(b, h, p)
    return o, lse


def _dilated_attention(q, k, v, rel_bias):
    b, s, h, dh = q.shape
    p = (s + PAD_UNIT - 1) // PAD_UNIT * PAD_UNIT

    def to_bhpd(t):
        t = jnp.pad(t.astype(jnp.float32), ((0, 0), (0, p - s), (0, 0), (0, 0)))
        return t.transpose(0, 2, 1, 3)

    q, k, v = to_bhpd(q), to_bhpd(k), to_bhpd(v)
    outs, lses = [], []
    for window, dilation in DILATED_BRANCHES:
        o_i, lse_i = _dilated_branch(q, k, v, rel_bias, window, dilation)
        outs.append(o_i)
        lses.append(lse_i)
    w = jax.nn.softmax(jnp.stack(lses), axis=0)
    o = jnp.sum(w[..., None] * jnp.stack(outs), axis=0)
    return o[:, :, :s].transpose(0, 2, 1, 3).reshape(b, s, h * dh)


def _causal_conv(x, w):
    return lax.conv_general_dilated(x, w[:, None, :], window_strides=(1,),
                                    padding=((CONV_WIDTH - 1, 0),),
                                    dimension_numbers=('NWC', 'WIO', 'NWC'),
                                    feature_group_count=x.shape[-1])


def _chunk_gated_delta_rule(q, k, v, g, beta):
    b, s, h, dk = q.shape
    dv = v.shape[-1]
    nc = s // CHUNK

    def chunks(t):
        return t.reshape(b, nc, CHUNK, h, t.shape[-1]).transpose(1, 0, 3, 2, 4)

    qc, kc, vc = chunks(q), chunks(k), chunks(v)
    gcum = jnp.cumsum(g.reshape(b, nc, CHUNK, h).transpose(1, 0, 3, 2), axis=-1)
    bc = beta.reshape(b, nc, CHUNK, h).transpose(1, 0, 3, 2)
    causal = jnp.tril(jnp.ones((CHUNK, CHUNK), dtype=bool))
    strict = jnp.tril(jnp.ones((CHUNK, CHUNK), dtype=bool), k=-1)
    diff = gcum[..., :, None] - gcum[..., None, :]
    decay = jnp.where(causal, jnp.exp(jnp.where(causal, diff, 0.0)), 0.0)
    k_beta = kc * bc[..., None]
    a_mat = jnp.where(strict, jnp.einsum('nbhcd,nbhed->nbhce', k_beta, kc) * decay, 0.0)
    rhs = jnp.concatenate([vc * bc[..., None], k_beta * jnp.exp(gcum)[..., None]], axis=-1)
    sol = lax.linalg.triangular_solve(a_mat + jnp.eye(CHUNK, dtype=a_mat.dtype), rhs,
                                      left_side=True, lower=True, unit_diagonal=True)
    u, w = sol[..., :dv], sol[..., dv:]
    qk = jnp.where(causal, jnp.einsum('nbhcd,nbhed->nbhce', qc, kc) * decay, 0.0)

    def step(state, xs):
        q_i, k_i, u_i, w_i, g_i, qk_i = xs
        v_new = u_i - jnp.einsum('bhck,bhkv->bhcv', w_i, state)
        o_i = (jnp.einsum('bhck,bhkv->bhcv', q_i * jnp.exp(g_i)[..., None], state)
               + jnp.einsum('bhce,bhev->bhcv', qk_i, v_new))
        g_last = g_i[..., -1]
        k_dec = k_i * jnp.exp(g_last[..., None] - g_i)[..., None]
        state = state * jnp.exp(g_last)[..., None, None] + jnp.einsum('bhck,bhcv->bhkv', k_dec, v_new)
        return state, o_i

    state0 = jnp.zeros((b, h, dk, dv), jnp.float32)
    _, o = lax.scan(step, state0, (qc, kc, u, w, gcum, qk))
    return o.transpose(1, 0, 3, 2, 4).reshape(b, s, h, dv)


def _gated_deltanet(q, k, v, z, b_logit, a_logit, conv_w, a_log, dt_bias, norm_g):
    bsz, s, _ = q.shape
    qkv = jax.nn.silu(_causal_conv(jnp.concatenate([q, k, v], axis=-1), conv_w))
    q, k, v = jnp.split(qkv.astype(jnp.float32), 3, axis=-1)
    shp = (bsz, s, N_DELTA_HEADS, HEAD_DIM)
    q = _l2norm(q.reshape(shp)) * HEAD_DIM ** -0.5
    k = _l2norm(k.reshape(shp))
    v = v.reshape(shp)
    beta = jax.nn.sigmoid(b_logit.astype(jnp.float32))
    g = -jnp.exp(a_log.astype(jnp.float32)) * jax.nn.softplus(
        a_logit.astype(jnp.float32) + dt_bias.astype(jnp.float32))
    o = _chunk_gated_delta_rule(q, k, v, g, beta)
    o = _rmsnorm(o, norm_g) * jax.nn.silu(z.astype(jnp.float32).reshape(shp))
    return o.reshape(bsz, s, DELTA_WIDTH).astype(z.dtype)


def setup_inputs(seed: int = 0) -> dict:
    key = jax.random.key(seed)
    ks = jax.random.split(key, 20)
    f32 = jnp.float32

    def nrm(k, shape, scale):
        return jax.random.normal(k, shape, f32) * scale

    dt = jnp.exp(jax.random.uniform(ks[9], (DEPTH, N_DELTA_HEADS), f32,
                                    minval=math.log(1e-3), maxval=math.log(1e-1)))
    return {
        'x': nrm(ks[0], (BATCH, SEQ, D_MODEL), 1.0),
        'c': nrm(ks[1], (BATCH, D_MODEL), 1.0),
        'w_ada': nrm(ks[2], (DEPTH, D_MODEL, 6 * D_MODEL), 0.5 * D_MODEL ** -0.5),
        'b_ada': nrm(ks[3], (DEPTH, 6 * D_MODEL), 0.02),
        'norm_attn_g': 1.0 + nrm(ks[4], (DEPTH, D_MODEL), 0.05),
        'w_in': nrm(ks[5], (DEPTH, D_MODEL, IN_WIDTH), D_MODEL ** -0.5),
        'rel_bias': nrm(ks[6], (N_BUCKETS, N_ATTN_HEADS), 0.5),
        'conv_w': nrm(ks[7], (DEPTH, CONV_WIDTH, 3 * DELTA_WIDTH), CONV_WIDTH ** -0.5),
        'a_log': jnp.log(jax.random.uniform(ks[8], (DEPTH, N_DELTA_HEADS), f32, minval=1.0, maxval=16.0)),
        'dt_bias': dt + jnp.log(-jnp.expm1(-dt)),
        'delta_norm_g': 1.0 + nrm(ks[10], (DEPTH, HEAD_DIM), 0.05),
        'w_out': nrm(ks[11], (DEPTH, MIX_WIDTH, D_MODEL), MIX_WIDTH ** -0.5),
        'norm_ffn_g': 1.0 + nrm(ks[12], (DEPTH, D_MODEL), 0.05),
        'w_gate': nrm(ks[13], (DEPTH, D_MODEL, D_FF), D_MODEL ** -0.5),
        'w_up': nrm(ks[14], (DEPTH, D_MODEL, D_FF), D_MODEL ** -0.5),
        'w_down': nrm(ks[15], (DEPTH, D_FF, D_MODEL), D_FF ** -0.5),
        'final_norm_g': 1.0 + nrm(ks[16], (D_MODEL,), 0.05),
    }


def reference(x, c, w_ada, b_ada, norm_attn_g, w_in, rel_bias, conv_w, a_log, dt_bias,
              delta_norm_g, w_out, norm_ffn_g, w_gate, w_up, w_down, final_norm_g):
    bsz, s, _ = x.shape
    split_points = np.cumsum([ATTN_WIDTH] * 3 + [DELTA_WIDTH] * 4 + [N_DELTA_HEADS])
    c_act = jax.nn.silu(c)
    for l in range(DEPTH):
        mod = c_act @ w_ada[l] + b_ada[l]
        sh1, sc1, g1, sh2, sc2, g2 = [m[:, None, :] for m in jnp.split(mod, 6, axis=-1)]
        h = _rmsnorm(x, norm_attn_g[l]) * (1.0 + sc1) + sh1
        proj = h @ w_in[l]
        q_a, k_a, v_a, q_d, k_d, v_d, z_d, b_d, a_d = jnp.split(proj, split_points, axis=-1)
        hs = (bsz, s, N_ATTN_HEADS, HEAD_DIM)
        y_attn = _dilated_attention(q_a.reshape(hs), k_a.reshape(hs), v_a.reshape(hs), rel_bias).astype(x.dtype)
        y_delta = _gated_deltanet(q_d, k_d, v_d, z_d, b_d, a_d, conv_w[l], a_log[l], dt_bias[l], delta_norm_g[l])
        y = jnp.concatenate([y_attn, y_delta], axis=-1) @ w_out[l]
        x = x + g1 * y
        h = _rmsnorm(x, norm_ffn_g[l]) * (1.0 + sc2) + sh2
        y = (jax.nn.silu(h @ w_gate[l]) * (h @ w_up[l])) @ w_down[l]
        x = x + g2 * y
    return _rmsnorm(x, final_norm_g)
```

```python
import functools
import math

import jax
import jax.numpy as jnp
import numpy as np
from jax import lax
from jax.experimental import pallas as pl
from jax.experimental.pallas import tpu as pltpu

F32 = jnp.float32
BF16 = jnp.bfloat16

HEAD_DIM = 64
N_ATTN_HEADS = 8
N_DELTA_HEADS = 8
ATTN_WIDTH = N_ATTN_HEADS * HEAD_DIM
DELTA_WIDTH = N_DELTA_HEADS * HEAD_DIM
DILATED_BRANCHES = ((128, 1), (512, 4), (2048, 16))
BAND = 128
PAD_UNIT = 2048
N_BUCKETS = 32
MAX_DISTANCE = 2048
CONV_WIDTH = 4
CHUNK = 64
EPS = 1e-6
NEG_INF = -1e30

LANES = 128
VMEM_LIMIT_BYTES = 60000 * 1024

TOKEN_TILE = 512
FFN_TOKEN_TILE = 256
DELTA_TILE = 512
HEADS_PER_GROUP = 4
GROUP_W = HEADS_PER_GROUP * HEAD_DIM


def _dot(a, b):
    return jnp.dot(a, b, preferred_element_type=F32)


def _dot_nt(a, b):
    return lax.dot_general(a, b, (((1,), (1,)), ((), ())), preferred_element_type=F32)


def _split3(x):
    x1 = x.astype(BF16)
    r1 = x - x1.astype(F32)
    x2 = r1.astype(BF16)
    x3 = (r1 - x2.astype(F32)).astype(BF16)
    return x1, x2, x3


def _split2(x):
    x1 = x.astype(BF16)
    x2 = (x - x1.astype(F32)).astype(BF16)
    return x1, x2


def _sigmoid(x):
    return 1.0 / (1.0 + jnp.exp(-x))


def _silu(x):
    return x * _sigmoid(x)


def _head_of(idx):
    return lax.shift_right_logical(idx, int(math.log2(HEAD_DIM)))


def _within_head(idx):
    return lax.bitwise_and(idx, HEAD_DIM - 1)


def _compiler_params(semantics):
    return pltpu.CompilerParams(dimension_semantics=semantics, vmem_limit_bytes=VMEM_LIMIT_BYTES)


def _const_spec(shape):
    zeros = (0,) * len(shape)
    return pl.BlockSpec(shape, lambda *_: zeros, pipeline_mode=pl.Buffered(1))


def _mod_kernel(c_ref, w_ref, b_ref, o_ref):
    ca = _silu(c_ref[...])
    acc = b_ref[...]
    w = w_ref[...]
    for part in _split3(ca):
        for wpart in _split3(w):
            acc = acc + _dot(part, wpart)
    o_ref[...] = acc


def _mod_call(c, w_ada, b_ada):
    bsz, d = c.shape
    n = w_ada.shape[1]
    blk = d
    return pl.pallas_call(
        _mod_kernel,
        grid=(n // blk,),
        in_specs=[
            pl.BlockSpec((bsz, d), lambda j: (0, 0)),
            pl.BlockSpec((d, blk), lambda j: (0, j)),
            pl.BlockSpec((1, blk), lambda j: (0, j)),
        ],
        out_specs=pl.BlockSpec((bsz, blk), lambda j: (0, j)),
        out_shape=jax.ShapeDtypeStruct((bsz, n), F32),
        compiler_params=_compiler_params(("arbitrary",)),
        name="mod",
    )(c, w_ada, b_ada.reshape(1, n))


def _rmsnorm_rows(x, g):
    return x * lax.rsqrt(jnp.mean(x * x, axis=-1, keepdims=True) + EPS) * g


def _inproj_kernel(x_ref, mod_ref, g_ref, wa_ref, wd_ref, wba_ref, qkv_ref, dl_ref, ba_ref):
    x = x_ref[0]
    h = _rmsnorm_rows(x, g_ref[...]) * (1.0 + mod_ref[0, 1:2, :]) + mod_ref[0, 0:1, :]
    hb = h.astype(BF16)
    qkv_ref[0] = _dot(hb, wa_ref[...]).astype(BF16)
    dl_ref[0] = _dot(hb, wd_ref[...])
    ba_ref[0] = _dot(hb, wba_ref[...])


def _inproj_call(x, mod3, g, wa, wd, wba):
    bsz, s, d = x.shape
    tm = TOKEN_TILE
    na, nd, nba = wa.shape[1], wd.shape[1], wba.shape[1]
    return pl.pallas_call(
        _inproj_kernel,
        grid=(bsz, s // tm),
        in_specs=[
            pl.BlockSpec((1, tm, d), lambda b, i: (b, i, 0)),
            pl.BlockSpec((1, 6, d), lambda b, i: (b, 0, 0)),
            _const_spec((1, d)),
            _const_spec((d, na)),
            _const_spec((d, nd)),
            _const_spec((d, nba)),
        ],
        out_specs=[
            pl.BlockSpec((1, tm, na), lambda b, i: (b, i, 0)),
            pl.BlockSpec((1, tm, nd), lambda b, i: (b, i, 0)),
            pl.BlockSpec((1, tm, nba), lambda b, i: (b, i, 0)),
        ],
        out_shape=[
            jax.ShapeDtypeStruct((bsz, s, na), BF16),
            jax.ShapeDtypeStruct((bsz, s, nd), F32),
            jax.ShapeDtypeStruct((bsz, s, nba), F32),
        ],
        compiler_params=_compiler_params(("parallel", "parallel")),
        name="inproj",
    )(x, mod3, g, wa, wd, wba)


def _t5_bucket(distance):
    max_exact = N_BUCKETS // 2
    dist_f = jnp.maximum(distance, 1).astype(F32)
    large = max_exact + (jnp.log(dist_f / max_exact) / math.log(MAX_DISTANCE / max_exact)
                         * (N_BUCKETS - max_exact)).astype(jnp.int32)
    return jnp.where(distance < max_exact, distance, jnp.minimum(large, N_BUCKETS - 1))


def _bucket_index_tables():
    qi = jnp.arange(BAND)[:, None]
    kj = jnp.arange(2 * BAND)[None, :]
    steps = qi + BAND - kj
    in_window = (steps >= 0) & (steps <= BAND)
    tables = []
    for _, dilation in DILATED_BRANCHES:
        bucket = _t5_bucket(jnp.maximum(steps, 0) * dilation)
        v0 = jnp.where(in_window, bucket, -1)
        v1 = jnp.where(in_window & (kj >= BAND), bucket, -1)
        v2 = jnp.concatenate([v0[:, BAND:], jnp.full((BAND, BAND), -1, jnp.int32)], axis=1)
        tables.append(jnp.stack([v0, v1, v2]))
    return jnp.stack(tables).astype(jnp.int32)


def _bias_kernel(idx_ref, rb_ref, o_ref):
    idx = idx_ref[0]
    for h in range(N_ATTN_HEADS):
        acc = jnp.full(idx.shape, NEG_INF, F32)
        for b in range(N_BUCKETS):
            acc = jnp.where(idx == b, rb_ref[b, h], acc)
        o_ref[0, h] = acc


def _bias_call(idx_tables, rel_bias):
    nbr = idx_tables.shape[0]
    blk = idx_tables.shape[1:]
    return pl.pallas_call(
        _bias_kernel,
        grid=(nbr,),
        in_specs=[
            pl.BlockSpec((1,) + blk, lambda i: (i, 0, 0, 0)),
            pl.BlockSpec(memory_space=pltpu.SMEM),
        ],
        out_specs=pl.BlockSpec((1, N_ATTN_HEADS) + blk, lambda i: (i, 0, 0, 0, 0)),
        out_shape=jax.ShapeDtypeStruct((nbr, N_ATTN_HEADS) + blk, F32),
        compiler_params=_compiler_params(("arbitrary",)),
        name="bias",
    )(idx_tables, rel_bias.astype(F32))


def _attn_kernel(q_ref, k_ref, v_ref, bias_ref, o_ref, lse_ref, *, blocks_per_class, n_blocks):
    lane = lax.broadcasted_iota(jnp.int32, (BAND, LANES), 1)
    is_a = lane < HEAD_DIM

    def body(j, carry):
        r0 = pl.multiple_of(j * BAND, BAND)
        ks = pl.multiple_of(jnp.maximum(j - 1, 0) * BAND, BAND)
        q = q_ref[0, 0, pl.ds(r0, BAND), :]
        kw = k_ref[0, 0, pl.ds(ks, 2 * BAND), :]
        vw = v_ref[0, 0, pl.ds(ks, 2 * BAND), :]
        first = lax.bitwise_and(j, blocks_per_class - 1) == 0
        variant = jnp.where(first, jnp.where(j > 0, 1, 2), 0)
        outs = []
        for hh in range(2):
            qm = jnp.where(is_a if hh == 0 else jnp.logical_not(is_a), q, jnp.zeros_like(q))
            s = _dot_nt(qm, kw) * (HEAD_DIM ** -0.5) + bias_ref[0, hh, variant]
            m = jnp.max(s, axis=-1, keepdims=True)
            e = jnp.exp(s - m)
            denom = jnp.sum(e, axis=-1, keepdims=True)
            o = _dot(e.astype(BF16), vw) / denom
            outs.append((o, m + jnp.log(denom)))
        o_ref[0, 0, pl.ds(r0, BAND), :] = jnp.where(is_a, outs[0][0], outs[1][0])
        lse_ref[0, 0, pl.ds(r0, BAND), :] = jnp.where(
            is_a, jnp.broadcast_to(outs[0][1], (BAND, LANES)), jnp.broadcast_to(outs[1][1], (BAND, LANES)))
        return carry

    lax.fori_loop(0, n_blocks, body, 0)


def _attn_call(q, k, v, bias, branch, dilation):
    bsz, hp, s, _ = q.shape
    n_blocks = s // BAND
    kern = functools.partial(_attn_kernel, blocks_per_class=n_blocks // dilation, n_blocks=n_blocks)
    seq_spec = pl.BlockSpec((1, 1, s, LANES), lambda h, b: (b, h, 0, 0))
    return pl.pallas_call(
        kern,
        grid=(hp, bsz),
        in_specs=[
            seq_spec, seq_spec, seq_spec,
            pl.BlockSpec((1, 2, 3, BAND, 2 * BAND), lambda h, b: (branch, h, 0, 0, 0)),
        ],
        out_specs=[seq_spec, seq_spec],
        out_shape=[jax.ShapeDtypeStruct((bsz, hp, s, LANES), F32)] * 2,
        compiler_params=_compiler_params(("parallel", "parallel")),
        name=f"attn_d{dilation}",
    )(q, k, v, bias)


def _to_class_major(t, dilation):
    bsz, s, w = t.shape
    hp = w // LANES
    t = t.reshape(bsz, s // dilation, dilation, hp, LANES).transpose(0, 3, 2, 1, 4)
    return t.reshape(bsz, hp, s, LANES)


def _from_class_major(t, dilation):
    bsz, hp, s, _ = t.shape
    t = t.reshape(bsz, hp, dilation, s // dilation, LANES).transpose(0, 3, 2, 1, 4)
    return t.reshape(bsz, s, hp * LANES)


def _bd_stack(x):
    xb = x.astype(BF16)
    head_of_lane = _head_of(lax.broadcasted_iota(jnp.int32, xb.shape, 1))
    zero = jnp.zeros_like(xb)
    return jnp.concatenate([jnp.where(head_of_lane == h, xb, zero) for h in range(HEADS_PER_GROUP)], axis=0)


def _delta_kernel(dl_ref, ba_ref, cw_ref, alog_ref, dtb_ref, ng_ref, o_ref, xe_ref, halo_ref, state_ref):
    tc = dl_ref.shape[1]
    qkv_w = 3 * DELTA_WIDTH
    n_groups = DELTA_WIDTH // GROUP_W

    @pl.when(pl.program_id(1) == 0)
    def _():
        halo_ref[...] = jnp.zeros_like(halo_ref)
        state_ref[...] = jnp.zeros_like(state_ref)

    xin = dl_ref[0, :, :qkv_w]
    xe_ref[0:8, :] = halo_ref[...]
    xe_ref[8:, :] = xin
    halo_ref[...] = xin[tc - 8:, :]
    conv = xin * cw_ref[CONV_WIDTH - 1:CONV_WIDTH, :]
    for back in range(1, CONV_WIDTH):
        conv = conv + xe_ref[8 - back:8 - back + tc, :] * cw_ref[CONV_WIDTH - 1 - back:CONV_WIDTH - back, :]
    act = _silu(conv)
    q_raw = act[:, :DELTA_WIDTH]
    k_raw = act[:, DELTA_WIDTH:2 * DELTA_WIDTH]
    v_all = act[:, 2 * DELTA_WIDTH:]

    r512 = _head_of(lax.broadcasted_iota(jnp.int32, (DELTA_WIDTH, DELTA_WIDTH), 0))
    c512 = _head_of(lax.broadcasted_iota(jnp.int32, (DELTA_WIDTH, DELTA_WIDTH), 1))
    head_ones = (r512 == c512).astype(BF16)

    def head_sum(x):
        hi, lo = _split2(x)
        return _dot(hi, head_ones) + _dot(lo, head_ones)

    q_all = q_raw * lax.rsqrt(head_sum(q_raw * q_raw) + EPS) * (HEAD_DIM ** -0.5)
    k_all = k_raw * lax.rsqrt(head_sum(k_raw * k_raw) + EPS)

    t = ba_ref[0]
    lane = lax.broadcasted_iota(jnp.int32, t.shape, 1)
    za = t + dtb_ref[...]
    softplus = jnp.maximum(za, 0.0) + jnp.log1p(jnp.exp(-jnp.abs(za)))
    g_lane = -jnp.exp(alog_ref[...]) * softplus
    comb = jnp.where(lane < N_DELTA_HEADS, _sigmoid(t), jnp.where(lane < 2 * N_DELTA_HEADS, g_lane, 0.0))
    er = lax.broadcasted_iota(jnp.int32, (LANES, 2 * DELTA_WIDTH), 0)
    ec = _head_of(lax.broadcasted_iota(jnp.int32, (LANES, 2 * DELTA_WIDTH), 1))
    expand = (er == ec).astype(BF16)
    bc = jnp.zeros((tc, 2 * DELTA_WIDTH), F32)
    for part in _split3(comb):
        bc = bc + _dot(part, expand)
    beta_all = bc[:, :DELTA_WIDTH]
    g_all = bc[:, DELTA_WIDTH:]

    ci = lax.broadcasted_iota(jnp.int32, (CHUNK, CHUNK), 0)
    cj = lax.broadcasted_iota(jnp.int32, (CHUNK, CHUNK), 1)
    lower_incl = (ci >= cj).astype(BF16)
    all_ones = jnp.ones((CHUNK, CHUNK), BF16)
    row_w = lax.broadcasted_iota(jnp.int32, (CHUNK, DELTA_WIDTH), 0)
    col_w = _within_head(lax.broadcasted_iota(jnp.int32, (CHUNK, DELTA_WIDTH), 1))
    upper_incl_w = row_w <= col_w
    row_g = lax.broadcasted_iota(jnp.int32, (CHUNK, GROUP_W), 0)
    col_g = _within_head(lax.broadcasted_iota(jnp.int32, (CHUNK, GROUP_W), 1))
    causal_g = row_g >= col_g
    strict_g = row_g > col_g
    eye_g = (row_g == col_g).astype(F32)
    level_masks = []
    for n in range(int(math.log2(CHUNK))):
        shr = lambda v, k: lax.shift_right_logical(v, k)
        level_masks.append((shr(row_g, n + 1) == shr(col_g, n + 1)) & (shr(row_g, n) > shr(col_g, n)))
    sr = _head_of(lax.broadcasted_iota(jnp.int32, (GROUP_W, GROUP_W), 0))
    sc = _head_of(lax.broadcasted_iota(jnp.int32, (GROUP_W, GROUP_W), 1))
    state_mask = sr == sc

    for c in range(tc // CHUNK):
        rows = slice(c * CHUNK, (c + 1) * CHUNK)
        gb = g_all[rows]
        gcum = jnp.zeros((CHUNK, DELTA_WIDTH), F32)
        gcum_t = jnp.zeros((CHUNK, DELTA_WIDTH), F32)
        for part in _split3(gb):
            gcum = gcum + _dot(lower_incl, part)
        for part in _split3(jnp.where(upper_incl_w, gb, 0.0)):
            gcum_t = gcum_t + _dot(all_ones, part)
        g_last = gcum[CHUNK - 1:CHUNK, :]
        exp_g = jnp.exp(gcum)
        exp_dec = jnp.exp(g_last - gcum)
        exp_last = jnp.exp(g_last)

        kc, qc, vc, bb = k_all[rows], q_all[rows], v_all[rows], beta_all[rows]
        kb = kc * bb
        vb = vc * bb
        kbe = kb * exp_g
        qe = qc * exp_g
        kdec = kc * exp_dec
        out_groups = []
        for g in range(n_groups):
            gl = slice(g * GROUP_W, (g + 1) * GROUP_W)
            diff = gcum[:, gl] - gcum_t[:, gl]
            decay = jnp.where(causal_g, jnp.exp(jnp.where(causal_g, diff, 0.0)), 0.0)
            aq = _dot_nt(jnp.concatenate([kb[:, gl], qc[:, gl]], axis=0).astype(BF16), _bd_stack(kc[:, gl]))
            a_mat = jnp.where(strict_g, aq[:CHUNK] * decay, 0.0)
            qk = aq[CHUNK:] * decay

            t_inv = eye_g - jnp.where(level_masks[0], a_mat, 0.0)
            for mask in level_masks[1:]:
                y = _dot(jnp.where(mask, a_mat, 0.0).astype(BF16), _bd_stack(t_inv))
                t_inv = t_inv - _dot(t_inv.astype(BF16), _bd_stack(y))
            uw = _dot(t_inv.astype(BF16), jnp.concatenate([_bd_stack(vb[:, gl]), _bd_stack(kbe[:, gl])], axis=1))
            u, w = uw[:, :GROUP_W], uw[:, GROUP_W:]

            state = state_ref[g]
            ws = _dot(jnp.concatenate([w, qe[:, gl]], axis=0).astype(BF16), state.astype(BF16))
            v_new = u - ws[:CHUNK]
            out_groups.append(ws[CHUNK:] + _dot(qk.astype(BF16), _bd_stack(v_new)))
            upd = _dot(kdec[:, gl].T.astype(BF16), v_new.astype(BF16))
            state_ref[g] = state * exp_last[:, gl] + jnp.where(state_mask, upd, 0.0)

        o_c = jnp.concatenate(out_groups, axis=1)
        ms = head_sum(o_c * o_c) * (1.0 / HEAD_DIM)
        o_ref[0, rows, :] = o_c * lax.rsqrt(ms + EPS) * ng_ref[...] * _silu(dl_ref[0, rows, qkv_w:])


def _delta_call(dl, ba, conv_w, alog_row, dtb_row, ng_row):
    bsz, s, wd = dl.shape
    tc = DELTA_TILE
    n_groups = DELTA_WIDTH // GROUP_W
    return pl.pallas_call(
        _delta_kernel,
        grid=(bsz, s // tc),
        in_specs=[
            pl.BlockSpec((1, tc, wd), lambda b, i: (b, i, 0)),
            pl.BlockSpec((1, tc, LANES), lambda b, i: (b, i, 0)),
            _const_spec(conv_w.shape),
            _const_spec((1, LANES)),
            _const_spec((1, LANES)),
            _const_spec((1, DELTA_WIDTH)),
        ],
        out_specs=pl.BlockSpec((1, tc, DELTA_WIDTH), lambda b, i: (b, i, 0)),
        out_shape=jax.ShapeDtypeStruct((bsz, s, DELTA_WIDTH), F32),
        scratch_shapes=[
            pltpu.VMEM((tc + 8, 3 * DELTA_WIDTH), F32),
            pltpu.VMEM((8, 3 * DELTA_WIDTH), F32),
            pltpu.VMEM((n_groups, GROUP_W, GROUP_W), F32),
        ],
        compiler_params=_compiler_params(("parallel", "arbitrary")),
        name="delta",
    )(dl, ba, conv_w, alog_row, dtb_row, ng_row)


def _out_ffn_kernel(x_ref, o1_ref, o2_ref, o3_ref, l1_ref, l2_ref, l3_ref, yd_ref, mod_ref,
                    wtop_ref, wbot_ref, g2_ref, wg_ref, wu_ref, wdn_ref, gf_ref, out_ref, *, final_norm):
    lses = [l1_ref[0], l2_ref[0], l3_ref[0]]
    outs = [o1_ref[0], o2_ref[0], o3_ref[0]]
    mx = jnp.maximum(jnp.maximum(lses[0], lses[1]), lses[2])
    ws = [jnp.exp(l - mx) for l in lses]
    den = ws[0] + ws[1] + ws[2]
    ya = (ws[0] * outs[0] + ws[1] * outs[1] + ws[2] * outs[2]) / den
    y = _dot(ya.astype(BF16), wtop_ref[...]) + _dot(yd_ref[0].astype(BF16), wbot_ref[...])
    x1 = x_ref[0] + mod_ref[0, 2:3, :] * y
    h = _rmsnorm_rows(x1, g2_ref[...]) * (1.0 + mod_ref[0, 4:5, :]) + mod_ref[0, 3:4, :]
    hb = h.astype(BF16)
    gate = _dot(hb, wg_ref[...])
    up = _dot(hb, wu_ref[...])
    y2 = _dot((_silu(gate) * up).astype(BF16), wdn_ref[...])
    x2 = x1 + mod_ref[0, 5:6, :] * y2
    out_ref[0] = _rmsnorm_rows(x2, gf_ref[...]) if final_norm else x2


def _out_ffn_call(x, o_list, lse_list, yd, mod3, wtop, wbot, g2, wg, wu, wdn, gf, final_norm):
    bsz, s, d = x.shape
    tm = FFN_TOKEN_TILE
    aw = o_list[0].shape[-1]
    dff = wg.shape[1]
    row_spec = lambda w: pl.BlockSpec((1, tm, w), lambda b, i: (b, i, 0))
    return pl.pallas_call(
        functools.partial(_out_ffn_kernel, final_norm=final_norm),
        grid=(bsz, s // tm),
        in_specs=[row_spec(d)] + [row_spec(aw)] * 6 + [row_spec(yd.shape[-1])] + [
            pl.BlockSpec((1, 6, d), lambda b, i: (b, 0, 0)),
            _const_spec(wtop.shape), _const_spec(wbot.shape), _const_spec((1, d)),
            _const_spec((d, dff)), _const_spec((d, dff)), _const_spec((dff, d)), _const_spec((1, d)),
        ],
        out_specs=row_spec(d),
        out_shape=jax.ShapeDtypeStruct((bsz, s, d), x.dtype),
        compiler_params=_compiler_params(("parallel", "parallel")),
        name="out_ffn",
    )(x, *o_list, *lse_list, yd, mod3, wtop, wbot, g2, wg, wu, wdn, gf)


def kernel(x, c, w_ada, b_ada, norm_attn_g, w_in, rel_bias, conv_w, a_log, dt_bias, delta_norm_g, w_out,
           norm_ffn_g, w_gate, w_up, w_down, final_norm_g):
    bsz, s, d = x.shape
    depth = w_ada.shape[0]
    assert s % PAD_UNIT == 0 and s % DELTA_TILE == 0 and s % TOKEN_TILE == 0
    assert all(window // dilation == BAND for window, dilation in DILATED_BRANCHES)

    bias = _bias_call(_bucket_index_tables(), rel_bias)
    lane_pad = LANES - 2 * N_DELTA_HEADS
    for l in range(depth):
        mod3 = _mod_call(c, w_ada[l], b_ada[l]).reshape(bsz, 6, d)
        qkv_w = 3 * ATTN_WIDTH
        dl_w = 4 * DELTA_WIDTH
        wa = w_in[l][:, :qkv_w].astype(BF16)
        wd = w_in[l][:, qkv_w:qkv_w + dl_w].astype(BF16)
        wba = jnp.pad(w_in[l][:, qkv_w + dl_w:], ((0, 0), (0, lane_pad))).astype(BF16)
        qkv, dl, ba = _inproj_call(x, mod3, norm_attn_g[l].reshape(1, d), wa, wd, wba)

        o_list, lse_list = [], []
        for branch, (_, dilation) in enumerate(DILATED_BRANCHES):
            parts = [_to_class_major(qkv[:, :, i * ATTN_WIDTH:(i + 1) * ATTN_WIDTH], dilation) for i in range(3)]
            o, lse = _attn_call(*parts, bias, branch, dilation)
            o_list.append(_from_class_major(o, dilation))
            lse_list.append(_from_class_major(lse, dilation))

        gate_pad = lambda v: jnp.pad(v.astype(F32), (N_DELTA_HEADS, LANES - 2 * N_DELTA_HEADS)).reshape(1, LANES)
        yd = _delta_call(dl, ba, conv_w[l].astype(F32), gate_pad(a_log[l]), gate_pad(dt_bias[l]),
                         jnp.tile(delta_norm_g[l].astype(F32), N_DELTA_HEADS).reshape(1, DELTA_WIDTH))

        wo = w_out[l].astype(BF16)
        x = _out_ffn_call(
            x, o_list, lse_list, yd, mod3, wo[:ATTN_WIDTH], wo[ATTN_WIDTH:], norm_ffn_g[l].reshape(1, d),
            w_gate[l].astype(BF16), w_up[l].astype(BF16), w_down[l].astype(BF16), final_norm_g.reshape(1, d),
            final_norm=(l == depth - 1))
    return x
```

```python
import functools
import math

import jax
import jax.numpy as jnp
import numpy as np
from jax import lax
from jax.experimental import pallas as pl
from jax.experimental.pallas import tpu as pltpu

F32 = jnp.float32
BF16 = jnp.bfloat16

HEAD_DIM = 64
N_ATTN_HEADS = 8
N_DELTA_HEADS = 8
ATTN_WIDTH = N_ATTN_HEADS * HEAD_DIM
DELTA_WIDTH = N_DELTA_HEADS * HEAD_DIM
DILATED_BRANCHES = ((128, 1), (512, 4), (2048, 16))
BAND = 128
PAD_UNIT = 2048
N_BUCKETS = 32
MAX_DISTANCE = 2048
CONV_WIDTH = 4
CHUNK = 64
EPS = 1e-6
NEG_INF = -1e30

LANES = 128
VMEM_LIMIT_BYTES = 60000 * 1024

TOKEN_TILE = 512
HALO = 8
FFN_TOKEN_TILE = 256
DELTA_TILE = 512
ATTN_UNROLL = 4
HEADS_PER_GROUP = 4
GROUP_W = HEADS_PER_GROUP * HEAD_DIM


def _dot(a, b):
    return jnp.dot(a, b, preferred_element_type=F32)


def _dot_nt(a, b):
    return lax.dot_general(a, b, (((1,), (1,)), ((), ())), preferred_element_type=F32)


def _split3(x):
    x1 = x.astype(BF16)
    r1 = x - x1.astype(F32)
    x2 = r1.astype(BF16)
    x3 = (r1 - x2.astype(F32)).astype(BF16)
    return x1, x2, x3


def _split2(x):
    x1 = x.astype(BF16)
    x2 = (x - x1.astype(F32)).astype(BF16)
    return x1, x2


def _sigmoid(x):
    return 1.0 / (1.0 + jnp.exp(-x))


def _silu(x):
    return x * _sigmoid(x)


def _head_of(idx):
    return lax.shift_right_logical(idx, int(math.log2(HEAD_DIM)))


def _within_head(idx):
    return lax.bitwise_and(idx, HEAD_DIM - 1)


def _compiler_params(semantics):
    return pltpu.CompilerParams(dimension_semantics=semantics, vmem_limit_bytes=VMEM_LIMIT_BYTES)


def _const_spec(shape):
    zeros = (0,) * len(shape)
    return pl.BlockSpec(shape, lambda *_: zeros, pipeline_mode=pl.Buffered(1))


def _mod_kernel(c_ref, w_ref, b_ref, o_ref):
    ca = _silu(c_ref[...])
    acc = b_ref[...]
    w = w_ref[...]
    for part in _split3(ca):
        for wpart in _split3(w):
            acc = acc + _dot(part, wpart)
    o_ref[...] = acc


def _mod_call(c, w_ada, b_ada):
    bsz, d = c.shape
    n = w_ada.shape[1]
    blk = d
    return pl.pallas_call(
        _mod_kernel,
        grid=(n // blk,),
        in_specs=[
            pl.BlockSpec((bsz, d), lambda j: (0, 0)),
            pl.BlockSpec((d, blk), lambda j: (0, j)),
            pl.BlockSpec((1, blk), lambda j: (0, j)),
        ],
        out_specs=pl.BlockSpec((bsz, blk), lambda j: (0, j)),
        out_shape=jax.ShapeDtypeStruct((bsz, n), F32),
        compiler_params=_compiler_params(("arbitrary",)),
        name="mod",
    )(c, w_ada, b_ada.reshape(1, n))


def _rmsnorm_rows(x, g):
    return x * lax.rsqrt(jnp.mean(x * x, axis=-1, keepdims=True) + EPS) * g


def _inproj_kernel(x_ref, xh_ref, mod_ref, g_ref, wa_ref, wc_ref, wz_ref, wba_ref, cw_ref,
                   qkv1_ref, qkv4_ref, qkv16_ref, dl_ref, ba_ref, stage_ref, pre_ref):
    tm = x_ref.shape[1]
    conv_w = wc_ref.shape[1]
    xe = jnp.concatenate([xh_ref[0], x_ref[0]], axis=0)
    h = _rmsnorm_rows(xe, g_ref[...]) * (1.0 + mod_ref[0, 1:2, :]) + mod_ref[0, 0:1, :]
    hb_ext = h.astype(BF16)
    hb = hb_ext[HALO:]

    qkv = _dot(hb, wa_ref[...])
    n_slabs = qkv.shape[1] // LANES
    for j in range(n_slabs):
        slab = qkv[:, j * LANES:(j + 1) * LANES]
        stage_ref[j] = slab
        qkv1_ref[0, j] = slab.astype(BF16)
    for out_ref, dilation in ((qkv4_ref, DILATED_BRANCHES[1][1]), (qkv16_ref, DILATED_BRANCHES[2][1])):
        for j in range(n_slabs):
            for r in range(dilation):
                out_ref[0, j, r] = stage_ref[j, pl.ds(r, tm // dilation, stride=dilation), :].astype(BF16)

    pre = _dot(hb_ext, wc_ref[...])
    row = lax.broadcasted_iota(jnp.int32, pre.shape, 0)
    pre = jnp.where((row < HALO) & (pl.program_id(1) == 0), 0.0, pre)
    pre_ref[...] = pre
    conv = pre[HALO:] * cw_ref[CONV_WIDTH - 1:CONV_WIDTH, :]
    for back in range(1, CONV_WIDTH):
        conv = conv + pre_ref[HALO - back:HALO - back + tm, :] * cw_ref[CONV_WIDTH - 1 - back:CONV_WIDTH - back, :]
    dl_ref[0, :, :conv_w] = _silu(conv)
    dl_ref[0, :, conv_w:] = _dot(hb, wz_ref[...])
    ba_ref[0] = _dot(hb, wba_ref[...])


def _inproj_call(x, mod3, g, wa, wc, wz, wba, conv_w):
    bsz, s, d = x.shape
    tm = TOKEN_TILE
    na, nc, nz, nba = wa.shape[1], wc.shape[1], wz.shape[1], wba.shape[1]
    n_slabs = na // LANES
    d4, d16 = DILATED_BRANCHES[1][1], DILATED_BRANCHES[2][1]
    row_spec = lambda w: pl.BlockSpec((1, tm, w), lambda b, i: (b, i, 0))
    cm_spec = lambda dil: pl.BlockSpec((1, n_slabs, dil, tm // dil, LANES), lambda b, i: (b, 0, 0, i, 0))
    cm_shape = lambda dil: jax.ShapeDtypeStruct((bsz, n_slabs, dil, s // dil, LANES), BF16)
    return pl.pallas_call(
        _inproj_kernel,
        grid=(bsz, s // tm),
        in_specs=[
            row_spec(d),
            pl.BlockSpec((1, HALO, d), lambda b, i: (b, jnp.maximum(i * (tm // HALO) - 1, 0), 0)),
            pl.BlockSpec((1, 6, d), lambda b, i: (b, 0, 0)),
            _const_spec((1, d)),
            _const_spec((d, na)),
            _const_spec((d, nc)),
            _const_spec((d, nz)),
            _const_spec((d, nba)),
            _const_spec(conv_w.shape),
        ],
        out_specs=[
            pl.BlockSpec((1, n_slabs, tm, LANES), lambda b, i: (b, 0, i, 0)),
            cm_spec(d4),
            cm_spec(d16),
            row_spec(nc + nz),
            row_spec(nba),
        ],
        out_shape=[
            jax.ShapeDtypeStruct((bsz, n_slabs, s, LANES), BF16),
            cm_shape(d4),
            cm_shape(d16),
            jax.ShapeDtypeStruct((bsz, s, nc + nz), F32),
            jax.ShapeDtypeStruct((bsz, s, nba), F32),
        ],
        scratch_shapes=[
            pltpu.VMEM((n_slabs, tm, LANES), F32),
            pltpu.VMEM((tm + HALO, nc), F32),
        ],
        compiler_params=_compiler_params(("parallel", "parallel")),
        name="inproj",
    )(x, x, mod3, g, wa, wc, wz, wba, conv_w)


def _t5_bucket(distance):
    max_exact = N_BUCKETS // 2
    dist_f = jnp.maximum(distance, 1).astype(F32)
    large = max_exact + (jnp.log(dist_f / max_exact) / math.log(MAX_DISTANCE / max_exact)
                         * (N_BUCKETS - max_exact)).astype(jnp.int32)
    return jnp.where(distance < max_exact, distance, jnp.minimum(large, N_BUCKETS - 1))


def _bucket_index_tables():
    qi = jnp.arange(BAND)[:, None]
    kj = jnp.arange(2 * BAND)[None, :]
    steps = qi + BAND - kj
    in_window = (steps >= 0) & (steps <= BAND)
    tables = []
    for _, dilation in DILATED_BRANCHES:
        bucket = _t5_bucket(jnp.maximum(steps, 0) * dilation)
        v0 = jnp.where(in_window, bucket, -1)
        v1 = jnp.where(in_window & (kj >= BAND), bucket, -1)
        v2 = jnp.concatenate([v0[:, BAND:], jnp.full((BAND, BAND), -1, jnp.int32)], axis=1)
        tables.append(jnp.stack([v0, v1, v2]))
    return jnp.stack(tables).astype(jnp.int32)


def _bias_kernel(idx_ref, rb_ref, o_ref):
    idx = idx_ref[0]
    for h in range(N_ATTN_HEADS):
        acc = jnp.full(idx.shape, NEG_INF, F32)
        for b in range(N_BUCKETS):
            acc = jnp.where(idx == b, rb_ref[b, h], acc)
        o_ref[0, h] = acc


def _bias_call(idx_tables, rel_bias):
    nbr = idx_tables.shape[0]
    blk = idx_tables.shape[1:]
    return pl.pallas_call(
        _bias_kernel,
        grid=(nbr,),
        in_specs=[
            pl.BlockSpec((1,) + blk, lambda i: (i, 0, 0, 0)),
            pl.BlockSpec(memory_space=pltpu.SMEM),
        ],
        out_specs=pl.BlockSpec((1, N_ATTN_HEADS) + blk, lambda i: (i, 0, 0, 0, 0)),
        out_shape=jax.ShapeDtypeStruct((nbr, N_ATTN_HEADS) + blk, F32),
        compiler_params=_compiler_params(("arbitrary",)),
        name="bias",
    )(idx_tables, rel_bias.astype(F32))


def _attn_kernel(q_ref, k_ref, v_ref, bias_ref, o_ref, lse_ref, *, dilation, blocks_per_class, n_blocks):
    lane = lax.broadcasted_iota(jnp.int32, (BAND, LANES), 1)
    is_a = lane < HEAD_DIM
    class_shift = int(math.log2(blocks_per_class))

    def body(jg, carry):
        blocks = []
        for u in range(ATTN_UNROLL):
            j = jg * ATTN_UNROLL + u
            r0 = pl.multiple_of(j * BAND, BAND)
            ks = pl.multiple_of(jnp.maximum(j - 1, 0) * BAND, BAND)
            n_in_class = lax.bitwise_and(j, blocks_per_class - 1)
            variant = jnp.where(n_in_class == 0, jnp.where(j > 0, 1, 2), 0)
            q = q_ref[0, 0, pl.ds(r0, BAND), :] * (HEAD_DIM ** -0.5)
            blocks.append(dict(
                q=q, kw=k_ref[0, 0, pl.ds(ks, 2 * BAND), :], vw=v_ref[0, 0, pl.ds(ks, 2 * BAND), :], variant=variant,
                row0=lax.shift_right_logical(j, class_shift) + dilation * BAND * n_in_class))
        chains = [(blk, hh) for blk in blocks for hh in range(2)]
        zero = jnp.zeros((BAND, LANES), BF16)
        s_all = [_dot_nt(jnp.where(is_a if hh == 0 else jnp.logical_not(is_a), blk["q"], zero), blk["kw"])
                 + bias_ref[0, hh, blk["variant"]] for blk, hh in chains]
        m_all = [jnp.max(s, axis=-1, keepdims=True) for s in s_all]
        e_all = [jnp.exp(s - m) for s, m in zip(s_all, m_all)]
        d_all = [jnp.sum(e, axis=-1, keepdims=True) for e in e_all]
        o_all = [_dot(e.astype(BF16), blk["vw"]) / d for e, d, (blk, _) in zip(e_all, d_all, chains)]
        lse_all = [jnp.broadcast_to(m + jnp.log(d), (BAND, LANES)) for m, d in zip(m_all, d_all)]
        for u, blk in enumerate(blocks):
            rows = pl.ds(blk["row0"], BAND, stride=dilation)
            o_ref[0, rows, :] = jnp.where(is_a, o_all[2 * u], o_all[2 * u + 1])
            lse_ref[0, rows, :] = jnp.where(is_a, lse_all[2 * u], lse_all[2 * u + 1])
        return carry

    lax.fori_loop(0, n_blocks // ATTN_UNROLL, body, 0)


def _attn_call(qkv, bias, branch, dilation):
    bsz, n_slabs, s, _ = qkv.shape
    hp = n_slabs // 3
    n_blocks = s // BAND
    blocks_per_class = n_blocks // dilation
    assert blocks_per_class & (blocks_per_class - 1) == 0 and n_blocks % ATTN_UNROLL == 0
    kern = functools.partial(_attn_kernel, dilation=dilation, blocks_per_class=blocks_per_class, n_blocks=n_blocks)
    slab_spec = lambda t: pl.BlockSpec((1, 1, s, LANES), lambda h, b: (b, t * hp + h, 0, 0))
    out_spec = pl.BlockSpec((1, s, LANES), lambda h, b: (b, 0, h))
    return pl.pallas_call(
        kern,
        grid=(hp, bsz),
        in_specs=[
            slab_spec(0), slab_spec(1), slab_spec(2),
            pl.BlockSpec((1, 2, 3, BAND, 2 * BAND), lambda h, b: (branch, h, 0, 0, 0)),
        ],
        out_specs=[out_spec, out_spec],
        out_shape=[jax.ShapeDtypeStruct((bsz, s, hp * LANES), F32)] * 2,
        compiler_params=_compiler_params(("parallel", "parallel")),
        name=f"attn_d{dilation}",
    )(qkv, qkv, qkv, bias)


def _bd_stack(x):
    xb = x.astype(BF16)
    head_of_lane = _head_of(lax.broadcasted_iota(jnp.int32, xb.shape, 1))
    zero = jnp.zeros_like(xb)
    return jnp.concatenate([jnp.where(head_of_lane == h, xb, zero) for h in range(HEADS_PER_GROUP)], axis=0)


def _delta_kernel(dl_ref, ba_ref, alog_ref, dtb_ref, ng_ref, o_ref, state_ref,
                  elast_ref, u_ref, wq_ref, qk_ref, kdt_ref, oraw_ref):
    tc = dl_ref.shape[1]
    qkv_w = 3 * DELTA_WIDTH
    n_groups = DELTA_WIDTH // GROUP_W

    @pl.when(pl.program_id(1) == 0)
    def _():
        state_ref[...] = jnp.zeros_like(state_ref)

    q_raw = dl_ref[0, :, :DELTA_WIDTH]
    k_raw = dl_ref[0, :, DELTA_WIDTH:2 * DELTA_WIDTH]
    v_all = dl_ref[0, :, 2 * DELTA_WIDTH:qkv_w]

    r512 = _head_of(lax.broadcasted_iota(jnp.int32, (DELTA_WIDTH, DELTA_WIDTH), 0))
    c512 = _head_of(lax.broadcasted_iota(jnp.int32, (DELTA_WIDTH, DELTA_WIDTH), 1))
    head_ones = (r512 == c512).astype(BF16)

    def head_sum(x):
        hi, lo = _split2(x)
        return _dot(hi, head_ones) + _dot(lo, head_ones)

    q_all = q_raw * lax.rsqrt(head_sum(q_raw * q_raw) + EPS) * (HEAD_DIM ** -0.5)
    k_all = k_raw * lax.rsqrt(head_sum(k_raw * k_raw) + EPS)

    t = ba_ref[0]
    lane = lax.broadcasted_iota(jnp.int32, t.shape, 1)
    za = t + dtb_ref[...]
    softplus = jnp.maximum(za, 0.0) + jnp.log1p(jnp.exp(-jnp.abs(za)))
    g_lane = -jnp.exp(alog_ref[...]) * softplus
    comb = jnp.where(lane < N_DELTA_HEADS, _sigmoid(t), jnp.where(lane < 2 * N_DELTA_HEADS, g_lane, 0.0))
    er = lax.broadcasted_iota(jnp.int32, (LANES, 2 * DELTA_WIDTH), 0)
    ec = _head_of(lax.broadcasted_iota(jnp.int32, (LANES, 2 * DELTA_WIDTH), 1))
    expand = (er == ec).astype(BF16)
    bc = jnp.zeros((tc, 2 * DELTA_WIDTH), F32)
    for part in _split3(comb):
        bc = bc + _dot(part, expand)
    beta_all = bc[:, :DELTA_WIDTH]
    g_all = bc[:, DELTA_WIDTH:]

    ci = lax.broadcasted_iota(jnp.int32, (CHUNK, CHUNK), 0)
    cj = lax.broadcasted_iota(jnp.int32, (CHUNK, CHUNK), 1)
    lower_incl = (ci >= cj).astype(BF16)
    all_ones = jnp.ones((CHUNK, CHUNK), BF16)
    row_w = lax.broadcasted_iota(jnp.int32, (CHUNK, DELTA_WIDTH), 0)
    col_w = _within_head(lax.broadcasted_iota(jnp.int32, (CHUNK, DELTA_WIDTH), 1))
    upper_incl_w = row_w <= col_w
    row_g = lax.broadcasted_iota(jnp.int32, (CHUNK, GROUP_W), 0)
    col_g = _within_head(lax.broadcasted_iota(jnp.int32, (CHUNK, GROUP_W), 1))
    causal_g = row_g >= col_g
    strict_g = row_g > col_g
    eye_g = (row_g == col_g).astype(F32)
    level_masks = []
    for n in range(int(math.log2(CHUNK))):
        shr = lambda v, k: lax.shift_right_logical(v, k)
        level_masks.append((shr(row_g, n + 1) == shr(col_g, n + 1)) & (shr(row_g, n) > shr(col_g, n)))
    sr = _head_of(lax.broadcasted_iota(jnp.int32, (GROUP_W, GROUP_W), 0))
    sc = _head_of(lax.broadcasted_iota(jnp.int32, (GROUP_W, GROUP_W), 1))
    state_mask = sr == sc

    n_chunks = tc // CHUNK
    chains = []
    for c in range(n_chunks):
        rows = slice(c * CHUNK, (c + 1) * CHUNK)
        gb = g_all[rows]
        gcum = jnp.zeros((CHUNK, DELTA_WIDTH), F32)
        gcum_t = jnp.zeros((CHUNK, DELTA_WIDTH), F32)
        for part in _split3(gb):
            gcum = gcum + _dot(lower_incl, part)
        for part in _split3(jnp.where(upper_incl_w, gb, 0.0)):
            gcum_t = gcum_t + _dot(all_ones, part)
        g_last = gcum[CHUNK - 1:CHUNK, :]
        exp_g = jnp.exp(gcum)
        exp_dec = jnp.exp(g_last - gcum)
        exp_last = jnp.exp(g_last)

        kc, qc, vc, bb = k_all[rows], q_all[rows], v_all[rows], beta_all[rows]
        kb = kc * bb
        vb = vc * bb
        kbe = kb * exp_g
        qe = qc * exp_g
        kdec = kc * exp_dec
        elast_ref[c] = jnp.broadcast_to(exp_last, (8, DELTA_WIDTH))
        for g in range(n_groups):
            gl = slice(g * GROUP_W, (g + 1) * GROUP_W)
            diff = gcum[:, gl] - gcum_t[:, gl]
            decay = jnp.where(causal_g, jnp.exp(jnp.where(causal_g, diff, 0.0)), 0.0)
            aq = _dot_nt(jnp.concatenate([kb[:, gl], qc[:, gl]], axis=0).astype(BF16), _bd_stack(kc[:, gl]))
            qk_ref[c, g] = (aq[CHUNK:] * decay).astype(BF16)
            kdt_ref[c, g] = kdec[:, gl].T.astype(BF16)
            chains.append(dict(
                c=c, g=g, a=jnp.where(strict_g, aq[:CHUNK] * decay, 0.0), qe=qe[:, gl],
                rhs=jnp.concatenate([_bd_stack(vb[:, gl]), _bd_stack(kbe[:, gl])], axis=1)))

    t_invs = [eye_g - jnp.where(level_masks[0], ch["a"], 0.0) for ch in chains]
    for mask in level_masks[1:]:
        ys = [_dot(jnp.where(mask, ch["a"], 0.0).astype(BF16), _bd_stack(t)) for ch, t in zip(chains, t_invs)]
        t_invs = [t - _dot(t.astype(BF16), _bd_stack(y)) for t, y in zip(t_invs, ys)]
    for ch, t in zip(chains, t_invs):
        uw = _dot(t.astype(BF16), ch["rhs"])
        u_ref[ch["c"], ch["g"]] = uw[:, :GROUP_W]
        wq_ref[ch["c"], ch["g"]] = jnp.concatenate([uw[:, GROUP_W:], ch["qe"]], axis=0).astype(BF16)

    def chunk_step(c, carry):
        elast = elast_ref[c]
        for g in range(n_groups):
            state = state_ref[g]
            ws = _dot(wq_ref[c, g], state.astype(BF16))
            v_new = u_ref[c, g] - ws[:CHUNK]
            r0 = pl.multiple_of(c * CHUNK, CHUNK)
            oraw_ref[pl.ds(r0, CHUNK), g * GROUP_W:(g + 1) * GROUP_W] = (
                ws[CHUNK:] + _dot(qk_ref[c, g], _bd_stack(v_new)))
            upd = _dot(kdt_ref[c, g], v_new.astype(BF16))
            state_ref[g] = state * elast[0:1, g * GROUP_W:(g + 1) * GROUP_W] + jnp.where(state_mask, upd, 0.0)
        return carry

    lax.fori_loop(0, n_chunks, chunk_step, 0)

    o_all = oraw_ref[...]
    ms = head_sum(o_all * o_all) * (1.0 / HEAD_DIM)
    o_ref[0] = o_all * lax.rsqrt(ms + EPS) * ng_ref[...] * _silu(dl_ref[0, :, qkv_w:])


def _delta_call(dl, ba, alog_row, dtb_row, ng_row):
    bsz, s, wd = dl.shape
    tc = DELTA_TILE
    n_groups = DELTA_WIDTH // GROUP_W
    n_chunks = tc // CHUNK
    return pl.pallas_call(
        _delta_kernel,
        grid=(bsz, s // tc),
        in_specs=[
            pl.BlockSpec((1, tc, wd), lambda b, i: (b, i, 0)),
            pl.BlockSpec((1, tc, LANES), lambda b, i: (b, i, 0)),
            _const_spec((1, LANES)),
            _const_spec((1, LANES)),
            _const_spec((1, DELTA_WIDTH)),
        ],
        out_specs=pl.BlockSpec((1, tc, DELTA_WIDTH), lambda b, i: (b, i, 0)),
        out_shape=jax.ShapeDtypeStruct((bsz, s, DELTA_WIDTH), F32),
        scratch_shapes=[
            pltpu.VMEM((n_groups, GROUP_W, GROUP_W), F32),
            pltpu.VMEM((n_chunks, 8, DELTA_WIDTH), F32),
            pltpu.VMEM((n_chunks, n_groups, CHUNK, GROUP_W), F32),
            pltpu.VMEM((n_chunks, n_groups, 2 * CHUNK, GROUP_W), BF16),
            pltpu.VMEM((n_chunks, n_groups, CHUNK, GROUP_W), BF16),
            pltpu.VMEM((n_chunks, n_groups, GROUP_W, CHUNK), BF16),
            pltpu.VMEM((tc, DELTA_WIDTH), F32),
        ],
        compiler_params=_compiler_params(("parallel", "arbitrary")),
        name="delta",
    )(dl, ba, alog_row, dtb_row, ng_row)


def _out_ffn_kernel(x_ref, o1_ref, o2_ref, o3_ref, l1_ref, l2_ref, l3_ref, yd_ref, mod_ref,
                    wtop_ref, wbot_ref, g2_ref, wg_ref, wu_ref, wdn_ref, gf_ref, out_ref, *, final_norm):
    lses = [l1_ref[0], l2_ref[0], l3_ref[0]]
    outs = [o1_ref[0], o2_ref[0], o3_ref[0]]
    mx = jnp.maximum(jnp.maximum(lses[0], lses[1]), lses[2])
    ws = [jnp.exp(l - mx) for l in lses]
    den = ws[0] + ws[1] + ws[2]
    ya = (ws[0] * outs[0] + ws[1] * outs[1] + ws[2] * outs[2]) / den
    y = _dot(ya.astype(BF16), wtop_ref[...]) + _dot(yd_ref[0].astype(BF16), wbot_ref[...])
    x1 = x_ref[0] + mod_ref[0, 2:3, :] * y
    h = _rmsnorm_rows(x1, g2_ref[...]) * (1.0 + mod_ref[0, 4:5, :]) + mod_ref[0, 3:4, :]
    hb = h.astype(BF16)
    gate = _dot(hb, wg_ref[...])
    up = _dot(hb, wu_ref[...])
    y2 = _dot((_silu(gate) * up).astype(BF16), wdn_ref[...])
    x2 = x1 + mod_ref[0, 5:6, :] * y2
    out_ref[0] = _rmsnorm_rows(x2, gf_ref[...]) if final_norm else x2


def _out_ffn_call(x, o_list, lse_list, yd, mod3, wtop, wbot, g2, wg, wu, wdn, gf, final_norm):
    bsz, s, d = x.shape
    tm = FFN_TOKEN_TILE
    aw = o_list[0].shape[-1]
    dff = wg.shape[1]
    row_spec = lambda w: pl.BlockSpec((1, tm, w), lambda b, i: (b, i, 0))
    return pl.pallas_call(
        functools.partial(_out_ffn_kernel, final_norm=final_norm),
        grid=(bsz, s // tm),
        in_specs=[row_spec(d)] + [row_spec(aw)] * 6 + [row_spec(yd.shape[-1])] + [
            pl.BlockSpec((1, 6, d), lambda b, i: (b, 0, 0)),
            _const_spec(wtop.shape), _const_spec(wbot.shape), _const_spec((1, d)),
            _const_spec((d, dff)), _const_spec((d, dff)), _const_spec((dff, d)), _const_spec((1, d)),
        ],
        out_specs=row_spec(d),
        out_shape=jax.ShapeDtypeStruct((bsz, s, d), x.dtype),
        compiler_params=_compiler_params(("parallel", "parallel")),
        name="out_ffn",
    )(x, *o_list, *lse_list, yd, mod3, wtop, wbot, g2, wg, wu, wdn, gf)


def kernel(x, c, w_ada, b_ada, norm_attn_g, w_in, rel_bias, conv_w, a_log, dt_bias, delta_norm_g, w_out,
           norm_ffn_g, w_gate, w_up, w_down, final_norm_g):
    bsz, s, d = x.shape
    depth = w_ada.shape[0]
    assert s % PAD_UNIT == 0 and s % DELTA_TILE == 0 and s % TOKEN_TILE == 0
    assert all(window // dilation == BAND for window, dilation in DILATED_BRANCHES)

    bias = _bias_call(_bucket_index_tables(), rel_bias)
    lane_pad = LANES - 2 * N_DELTA_HEADS
    for l in range(depth):
        mod3 = _mod_call(c, w_ada[l], b_ada[l]).reshape(bsz, 6, d)
        qkv_w = 3 * ATTN_WIDTH
        dl_w = 4 * DELTA_WIDTH
        conv_cols = 3 * DELTA_WIDTH
        wa = w_in[l][:, :qkv_w].astype(BF16)
        wc = w_in[l][:, qkv_w:qkv_w + conv_cols].astype(BF16)
        wz = w_in[l][:, qkv_w + conv_cols:qkv_w + dl_w].astype(BF16)
        wba = jnp.pad(w_in[l][:, qkv_w + dl_w:], ((0, 0), (0, lane_pad))).astype(BF16)
        *qkv_by_branch, dl, ba = _inproj_call(
            x, mod3, norm_attn_g[l].reshape(1, d), wa, wc, wz, wba, conv_w[l].astype(F32))

        o_list, lse_list = [], []
        for branch, ((_, dilation), qkv) in enumerate(zip(DILATED_BRANCHES, qkv_by_branch)):
            o, lse = _attn_call(qkv.reshape(bsz, qkv.shape[1], s, LANES), bias, branch, dilation)
            o_list.append(o)
            lse_list.append(lse)

        gate_pad = lambda v: jnp.pad(v.astype(F32), (N_DELTA_HEADS, LANES - 2 * N_DELTA_HEADS)).reshape(1, LANES)
        yd = _delta_call(dl, ba, gate_pad(a_log[l]), gate_pad(dt_bias[l]),
                         jnp.tile(delta_norm_g[l].astype(F32), N_DELTA_HEADS).reshape(1, DELTA_WIDTH))

        wo = w_out[l].astype(BF16)
        x = _out_ffn_call(
            x, o_list, lse_list, yd, mod3, wo[:ATTN_WIDTH], wo[ATTN_WIDTH:], norm_ffn_g[l].reshape(1, d),
            w_gate[l].astype(BF16), w_up[l].astype(BF16), w_down[l].astype(BF16), final_norm_g.reshape(1, d),
            final_norm=(l == depth - 1))
    return x
```

```python
import functools
import math

import jax
import jax.numpy as jnp
import numpy as np
from jax import lax
from jax.experimental import pallas as pl
from jax.experimental.pallas import tpu as pltpu

F32 = jnp.float32
BF16 = jnp.bfloat16

HEAD_DIM = 64
N_ATTN_HEADS = 8
N_DELTA_HEADS = 8
ATTN_WIDTH = N_ATTN_HEADS * HEAD_DIM
DELTA_WIDTH = N_DELTA_HEADS * HEAD_DIM
DILATED_BRANCHES = ((128, 1), (512, 4), (2048, 16))
BAND = 128
PAD_UNIT = 2048
N_BUCKETS = 32
MAX_DISTANCE = 2048
CONV_WIDTH = 4
CHUNK = 64
EPS = 1e-6
NEG_INF = -1e30

LANES = 128
MXU_COLS = 256
VMEM_LIMIT_BYTES = 60000 * 1024

TOKEN_TILE = 512
HALO = 16
FFN_TOKEN_TILE = 256
DELTA_CHUNKS_PER_STEP = 1
ATTN_UNROLL = 8
HEADS_PER_GROUP = 4
GROUP_W = HEADS_PER_GROUP * HEAD_DIM


def _dot(a, b):
    return jnp.dot(a, b, preferred_element_type=F32)


def _dot_nt(a, b):
    return lax.dot_general(a, b, (((1,), (1,)), ((), ())), preferred_element_type=F32)


def _split3(x):
    x1 = x.astype(BF16)
    r1 = x - x1.astype(F32)
    x2 = r1.astype(BF16)
    x3 = (r1 - x2.astype(F32)).astype(BF16)
    return x1, x2, x3


def _split2(x):
    x1 = x.astype(BF16)
    x2 = (x - x1.astype(F32)).astype(BF16)
    return x1, x2


def _sigmoid(x):
    return 1.0 / (1.0 + jnp.exp(-x))


def _silu(x):
    return x * _sigmoid(x)


def _head_of(idx):
    return lax.shift_right_logical(idx, int(math.log2(HEAD_DIM)))


def _within_head(idx):
    return lax.bitwise_and(idx, HEAD_DIM - 1)


def _compiler_params(semantics):
    return pltpu.CompilerParams(dimension_semantics=semantics, vmem_limit_bytes=VMEM_LIMIT_BYTES)


def _const_spec(shape):
    zeros = (0,) * len(shape)
    return pl.BlockSpec(shape, lambda *_: zeros, pipeline_mode=pl.Buffered(1))


def _mod_kernel(c_ref, w_ref, b_ref, o_ref):
    ca = _silu(c_ref[...])
    acc = b_ref[...]
    w = w_ref[...]
    for part in _split3(ca):
        for wpart in _split3(w):
            acc = acc + _dot(part, wpart)
    o_ref[...] = acc


def _mod_call(c, w_ada, b_ada):
    bsz, d = c.shape
    n = w_ada.shape[1]
    blk = d
    return pl.pallas_call(
        _mod_kernel,
        grid=(n // blk,),
        in_specs=[
            pl.BlockSpec((bsz, d), lambda j: (0, 0)),
            pl.BlockSpec((d, blk), lambda j: (0, j)),
            pl.BlockSpec((1, blk), lambda j: (0, j)),
        ],
        out_specs=pl.BlockSpec((bsz, blk), lambda j: (0, j)),
        out_shape=jax.ShapeDtypeStruct((bsz, n), F32),
        compiler_params=_compiler_params(("arbitrary",)),
        name="mod",
    )(c, w_ada, b_ada.reshape(1, n))


def _rmsnorm_rows(x, g):
    return x * lax.rsqrt(jnp.mean(x * x, axis=-1, keepdims=True) + EPS) * g


def _inproj_kernel(x_ref, xh_ref, mod_ref, g_ref, wa_ref, wc_ref, wz_ref, wba_ref, cw_ref,
                   qkv1_ref, qkv4_ref, qkv16_ref, dl_ref, ba_ref, hb_ref, stage_ref, pre_ref):
    tm = x_ref.shape[1]
    conv_w = wc_ref.shape[1]
    xe = jnp.concatenate([xh_ref[0], x_ref[0]], axis=0)
    h = _rmsnorm_rows(xe, g_ref[...]) * (1.0 + mod_ref[0, 1:2, :]) + mod_ref[0, 0:1, :]
    hb_ref[...] = h.astype(BF16)

    n_slabs = wa_ref.shape[1] // LANES
    for j0 in range(0, n_slabs, MXU_COLS // LANES):
        blk = _dot(hb_ref[HALO:, :], wa_ref[:, j0 * LANES:j0 * LANES + MXU_COLS])
        for j in range(j0, j0 + MXU_COLS // LANES):
            slab = blk[:, (j - j0) * LANES:(j - j0 + 1) * LANES]
            stage_ref[j] = slab
            qkv1_ref[0, j] = slab.astype(BF16)
            for out_ref, dilation in ((qkv4_ref, DILATED_BRANCHES[1][1]), (qkv16_ref, DILATED_BRANCHES[2][1])):
                for r in range(dilation):
                    out_ref[0, j, r] = stage_ref[j, pl.ds(r, tm // dilation, stride=dilation), :].astype(BF16)

    row = lax.broadcasted_iota(jnp.int32, (tm + HALO, MXU_COLS), 0)
    no_history = (row < HALO) & (pl.program_id(1) == 0)
    for c0 in range(0, conv_w, MXU_COLS):
        cols = slice(c0, c0 + MXU_COLS)
        pre = jnp.where(no_history, 0.0, _dot(hb_ref[...], wc_ref[:, cols]))
        pre_ref[:, cols] = pre
        conv = pre[HALO:] * cw_ref[CONV_WIDTH - 1:CONV_WIDTH, cols]
        for back in range(1, CONV_WIDTH):
            tap = cw_ref[CONV_WIDTH - 1 - back:CONV_WIDTH - back, cols]
            conv = conv + pre_ref[HALO - back:HALO - back + tm, cols] * tap
        dl_ref[0, :, cols] = _silu(conv)
    for c0 in range(0, wz_ref.shape[1], MXU_COLS):
        dl_ref[0, :, conv_w + c0:conv_w + c0 + MXU_COLS] = _dot(hb_ref[HALO:, :], wz_ref[:, c0:c0 + MXU_COLS])
    ba_ref[0] = _dot(hb_ref[HALO:, :], wba_ref[...])


def _inproj_call(x, mod3, g, wa, wc, wz, wba, conv_w):
    bsz, s, d = x.shape
    tm = TOKEN_TILE
    na, nc, nz, nba = wa.shape[1], wc.shape[1], wz.shape[1], wba.shape[1]
    n_slabs = na // LANES
    d4, d16 = DILATED_BRANCHES[1][1], DILATED_BRANCHES[2][1]
    row_spec = lambda w: pl.BlockSpec((1, tm, w), lambda b, i: (b, i, 0))
    cm_spec = lambda dil: pl.BlockSpec((1, n_slabs, dil, tm // dil, LANES), lambda b, i: (b, 0, 0, i, 0))
    cm_shape = lambda dil: jax.ShapeDtypeStruct((bsz, n_slabs, dil, s // dil, LANES), BF16)
    return pl.pallas_call(
        _inproj_kernel,
        grid=(bsz, s // tm),
        in_specs=[
            row_spec(d),
            pl.BlockSpec((1, HALO, d), lambda b, i: (b, jnp.maximum(i * (tm // HALO) - 1, 0), 0)),
            pl.BlockSpec((1, 6, d), lambda b, i: (b, 0, 0)),
            _const_spec((1, d)),
            _const_spec((d, na)),
            _const_spec((d, nc)),
            _const_spec((d, nz)),
            _const_spec((d, nba)),
            _const_spec(conv_w.shape),
        ],
        out_specs=[
            pl.BlockSpec((1, n_slabs, tm, LANES), lambda b, i: (b, 0, i, 0)),
            cm_spec(d4),
            cm_spec(d16),
            row_spec(nc + nz),
            row_spec(nba),
        ],
        out_shape=[
            jax.ShapeDtypeStruct((bsz, n_slabs, s, LANES), BF16),
            cm_shape(d4),
            cm_shape(d16),
            jax.ShapeDtypeStruct((bsz, s, nc + nz), F32),
            jax.ShapeDtypeStruct((bsz, s, nba), F32),
        ],
        scratch_shapes=[
            pltpu.VMEM((tm + HALO, d), BF16),
            pltpu.VMEM((n_slabs, tm, LANES), F32),
            pltpu.VMEM((tm + HALO, nc), F32),
        ],
        compiler_params=_compiler_params(("parallel", "parallel")),
        name="inproj",
    )(x, x, mod3, g, wa, wc, wz, wba, conv_w)


def _t5_bucket(distance):
    max_exact = N_BUCKETS // 2
    dist_f = jnp.maximum(distance, 1).astype(F32)
    large = max_exact + (jnp.log(dist_f / max_exact) / math.log(MAX_DISTANCE / max_exact)
                         * (N_BUCKETS - max_exact)).astype(jnp.int32)
    return jnp.where(distance < max_exact, distance, jnp.minimum(large, N_BUCKETS - 1))


def _bucket_index_tables():
    qi = jnp.arange(BAND)[:, None]
    kj = jnp.arange(2 * BAND)[None, :]
    steps = qi + BAND - kj
    in_window = (steps >= 0) & (steps <= BAND)
    tables = []
    for _, dilation in DILATED_BRANCHES:
        bucket = _t5_bucket(jnp.maximum(steps, 0) * dilation)
        v0 = jnp.where(in_window, bucket, -1)
        v1 = jnp.where(in_window & (kj >= BAND), bucket, -1)
        v2 = jnp.concatenate([v0[:, BAND:], jnp.full((BAND, BAND), -1, jnp.int32)], axis=1)
        tables.append(jnp.stack([v0, v1, v2]))
    return jnp.stack(tables).astype(jnp.int32)


def _bias_kernel(idx_ref, rb_ref, o_ref):
    idx = idx_ref[0]
    for h in range(N_ATTN_HEADS):
        acc = jnp.full(idx.shape, NEG_INF, F32)
        for b in range(N_BUCKETS):
            acc = jnp.where(idx == b, rb_ref[b, h], acc)
        o_ref[0, h] = acc


def _bias_call(idx_tables, rel_bias):
    nbr = idx_tables.shape[0]
    blk = idx_tables.shape[1:]
    return pl.pallas_call(
        _bias_kernel,
        grid=(nbr,),
        in_specs=[
            pl.BlockSpec((1,) + blk, lambda i: (i, 0, 0, 0)),
            pl.BlockSpec(memory_space=pltpu.SMEM),
        ],
        out_specs=pl.BlockSpec((1, N_ATTN_HEADS) + blk, lambda i: (i, 0, 0, 0, 0)),
        out_shape=jax.ShapeDtypeStruct((nbr, N_ATTN_HEADS) + blk, F32),
        compiler_params=_compiler_params(("arbitrary",)),
        name="bias",
    )(idx_tables, rel_bias.astype(F32))


def _attn_kernel(q_ref, k_ref, v_ref, bias_ref, o_ref, lse_ref, *, dilation, blocks_per_class, n_blocks):
    lane = lax.broadcasted_iota(jnp.int32, (BAND, LANES), 1)
    is_a = lane < HEAD_DIM
    class_shift = int(math.log2(blocks_per_class))

    def body(jg, carry):
        blocks = []
        for u in range(ATTN_UNROLL):
            j = jg * ATTN_UNROLL + u
            r0 = pl.multiple_of(j * BAND, BAND)
            ks = pl.multiple_of(jnp.maximum(j - 1, 0) * BAND, BAND)
            n_in_class = lax.bitwise_and(j, blocks_per_class - 1)
            variant = jnp.where(n_in_class == 0, jnp.where(j > 0, 1, 2), 0)
            q = q_ref[0, 0, pl.ds(r0, BAND), :] * (HEAD_DIM ** -0.5)
            blocks.append(dict(
                q=q, kw=k_ref[0, 0, pl.ds(ks, 2 * BAND), :], vw=v_ref[0, 0, pl.ds(ks, 2 * BAND), :], variant=variant,
                row0=lax.shift_right_logical(j, class_shift) + dilation * BAND * n_in_class))
        chains = [(blk, hh) for blk in blocks for hh in range(2)]
        zero = jnp.zeros((BAND, LANES), BF16)
        s_all = [_dot_nt(jnp.where(is_a if hh == 0 else jnp.logical_not(is_a), blk["q"], zero), blk["kw"])
                 + bias_ref[0, hh, blk["variant"]] for blk, hh in chains]
        m_all = [jnp.max(s, axis=-1, keepdims=True) for s in s_all]
        e_all = [jnp.exp(s - m) for s, m in zip(s_all, m_all)]
        d_all = [jnp.sum(e, axis=-1, keepdims=True) for e in e_all]
        o_all = [_dot(e.astype(BF16), blk["vw"]) / d for e, d, (blk, _) in zip(e_all, d_all, chains)]
        lse_all = [jnp.broadcast_to(m + jnp.log(d), (BAND, LANES)) for m, d in zip(m_all, d_all)]
        for u, blk in enumerate(blocks):
            rows = pl.ds(blk["row0"], BAND, stride=dilation)
            o_ref[0, rows, :] = jnp.where(is_a, o_all[2 * u], o_all[2 * u + 1])
            lse_ref[0, rows, :] = jnp.where(is_a, lse_all[2 * u], lse_all[2 * u + 1])
        return carry

    lax.fori_loop(0, n_blocks // ATTN_UNROLL, body, 0)


def _attn_call(qkv, bias, branch, dilation):
    bsz, n_slabs, s, _ = qkv.shape
    hp = n_slabs // 3
    n_blocks = s // BAND
    blocks_per_class = n_blocks // dilation
    assert blocks_per_class & (blocks_per_class - 1) == 0 and n_blocks % ATTN_UNROLL == 0
    kern = functools.partial(_attn_kernel, dilation=dilation, blocks_per_class=blocks_per_class, n_blocks=n_blocks)
    slab_spec = lambda t: pl.BlockSpec((1, 1, s, LANES), lambda h, b: (b, t * hp + h, 0, 0))
    out_spec = pl.BlockSpec((1, s, LANES), lambda h, b: (b, 0, h))
    return pl.pallas_call(
        kern,
        grid=(hp, bsz),
        in_specs=[
            slab_spec(0), slab_spec(1), slab_spec(2),
            pl.BlockSpec((1, 2, 3, BAND, 2 * BAND), lambda h, b: (branch, h, 0, 0, 0)),
        ],
        out_specs=[out_spec, out_spec],
        out_shape=[jax.ShapeDtypeStruct((bsz, s, hp * LANES), F32)] * 2,
        compiler_params=_compiler_params(("parallel", "parallel")),
        name=f"attn_d{dilation}",
    )(qkv, qkv, qkv, bias)


def _bd_stack(x):
    xb = x.astype(BF16)
    head_of_lane = _head_of(lax.broadcasted_iota(jnp.int32, xb.shape, 1))
    zero = jnp.zeros_like(xb)
    return jnp.concatenate([jnp.where(head_of_lane == h, xb, zero) for h in range(HEADS_PER_GROUP)], axis=0)


def _delta_kernel(dl_ref, ba_ref, alog_ref, dtb_ref, ng_ref, o_ref, state_ref):
    bsz, rows_per_batch = dl_ref.shape[0], dl_ref.shape[1]
    chunks_per_batch = rows_per_batch // CHUNK
    tc = bsz * rows_per_batch
    qkv_w = 3 * DELTA_WIDTH
    n_groups = DELTA_WIDTH // GROUP_W

    @pl.when(pl.program_id(0) == 0)
    def _():
        state_ref[...] = jnp.zeros_like(state_ref)

    q_raw = dl_ref[:, :, :DELTA_WIDTH].reshape(tc, DELTA_WIDTH)
    k_raw = dl_ref[:, :, DELTA_WIDTH:2 * DELTA_WIDTH].reshape(tc, DELTA_WIDTH)
    v_all = dl_ref[:, :, 2 * DELTA_WIDTH:qkv_w].reshape(tc, DELTA_WIDTH)

    r512 = _head_of(lax.broadcasted_iota(jnp.int32, (DELTA_WIDTH, DELTA_WIDTH), 0))
    c512 = _head_of(lax.broadcasted_iota(jnp.int32, (DELTA_WIDTH, DELTA_WIDTH), 1))
    head_ones = (r512 == c512).astype(BF16)

    def head_sum(x):
        return _dot(x.astype(BF16), head_ones)

    q_all = q_raw * lax.rsqrt(head_sum(q_raw * q_raw) + EPS) * (HEAD_DIM ** -0.5)
    k_all = k_raw * lax.rsqrt(head_sum(k_raw * k_raw) + EPS)

    t = ba_ref[...].reshape(tc, LANES)
    za = t + dtb_ref[...]
    softplus = jnp.maximum(za, 0.0) + jnp.log1p(jnp.exp(-jnp.abs(za)))
    g_lane = -jnp.exp(alog_ref[...]) * softplus
    er = lax.broadcasted_iota(jnp.int32, (LANES, DELTA_WIDTH), 0)
    ec = _head_of(lax.broadcasted_iota(jnp.int32, (LANES, DELTA_WIDTH), 1))
    expand_beta = (er == ec).astype(BF16)
    expand_g = (er == ec + N_DELTA_HEADS).astype(BF16)
    beta_all = _dot(_sigmoid(t).astype(BF16), expand_beta)
    g_all = jnp.zeros((tc, DELTA_WIDTH), F32)
    for part in _split3(g_lane):
        g_all = g_all + _dot(part, expand_g)

    ci = lax.broadcasted_iota(jnp.int32, (CHUNK, CHUNK), 0)
    cj = lax.broadcasted_iota(jnp.int32, (CHUNK, CHUNK), 1)
    lower_incl = (ci >= cj).astype(BF16)
    all_ones = jnp.ones((CHUNK, CHUNK), BF16)
    row_w = lax.broadcasted_iota(jnp.int32, (CHUNK, DELTA_WIDTH), 0)
    col_w = _within_head(lax.broadcasted_iota(jnp.int32, (CHUNK, DELTA_WIDTH), 1))
    upper_incl_w = row_w <= col_w
    row_g = lax.broadcasted_iota(jnp.int32, (CHUNK, GROUP_W), 0)
    col_g = _within_head(lax.broadcasted_iota(jnp.int32, (CHUNK, GROUP_W), 1))
    causal_g = row_g >= col_g
    strict_g = row_g > col_g
    eye_g = (row_g == col_g).astype(F32)
    level_masks = []
    for n in range(int(math.log2(CHUNK))):
        shr = lambda v, k: lax.shift_right_logical(v, k)
        level_masks.append((shr(row_g, n + 1) == shr(col_g, n + 1)) & (shr(row_g, n) > shr(col_g, n)))
    sr = _head_of(lax.broadcasted_iota(jnp.int32, (GROUP_W, GROUP_W), 0))
    sc = _head_of(lax.broadcasted_iota(jnp.int32, (GROUP_W, GROUP_W), 1))
    state_mask = sr == sc

    chains = []
    for c in range(bsz * chunks_per_batch):
        rows = slice(c * CHUNK, (c + 1) * CHUNK)
        gb = g_all[rows]
        gcum = jnp.zeros((CHUNK, DELTA_WIDTH), F32)
        gcum_t = jnp.zeros((CHUNK, DELTA_WIDTH), F32)
        for part in _split3(gb):
            gcum = gcum + _dot(lower_incl, part)
        for part in _split3(jnp.where(upper_incl_w, gb, 0.0)):
            gcum_t = gcum_t + _dot(all_ones, part)
        g_last = gcum[CHUNK - 1:CHUNK, :]
        exp_g = jnp.exp(gcum)
        exp_dec = jnp.exp(g_last - gcum)
        exp_last = jnp.exp(g_last)

        kc, qc, vc, bb = k_all[rows], q_all[rows], v_all[rows], beta_all[rows]
        kb = kc * bb
        vb = vc * bb
        kbe = kb * exp_g
        qe = qc * exp_g
        kdec = kc * exp_dec
        for g in range(n_groups):
            gl = slice(g * GROUP_W, (g + 1) * GROUP_W)
            diff = gcum[:, gl] - gcum_t[:, gl]
            decay = jnp.where(causal_g, jnp.exp(jnp.where(causal_g, diff, 0.0)), 0.0)
            aq = _dot_nt(jnp.concatenate([kb[:, gl], qc[:, gl]], axis=0).astype(BF16), _bd_stack(kc[:, gl]))
            chains.append(dict(
                slot=(c // chunks_per_batch) * n_groups + g, a=jnp.where(strict_g, aq[:CHUNK] * decay, 0.0),
                qk=(aq[CHUNK:] * decay).astype(BF16), qe=qe[:, gl], kdec_t=kdec[:, gl].T.astype(BF16),
                exp_last=exp_last[:, gl],
                rhs=jnp.concatenate([_bd_stack(vb[:, gl]), _bd_stack(kbe[:, gl])], axis=1)))

    t_invs = [eye_g - jnp.where(level_masks[0], ch["a"], 0.0) for ch in chains]
    for mask in level_masks[1:]:
        ys = [_dot(jnp.where(mask, ch["a"], 0.0).astype(BF16), _bd_stack(t)) for ch, t in zip(chains, t_invs)]
        t_invs = [t - _dot(t.astype(BF16), _bd_stack(y)) for t, y in zip(t_invs, ys)]
    uws = [_dot(t.astype(BF16), ch["rhs"]) for ch, t in zip(chains, t_invs)]

    n_slots = bsz * n_groups
    states = [state_ref[slot] for slot in range(n_slots)]
    outs = [None] * len(chains)
    for j in range(chunks_per_batch):
        idx = [(b * chunks_per_batch + j) * n_groups + g for b in range(bsz) for g in range(n_groups)]
        wss = [_dot(jnp.concatenate([uws[i][:, GROUP_W:], chains[i]["qe"]], axis=0).astype(BF16), st.astype(BF16))
               for i, st in zip(idx, states)]
        v_news = [uws[i][:, :GROUP_W] - ws[:CHUNK] for i, ws in zip(idx, wss)]
        for i, ws, vn in zip(idx, wss, v_news):
            outs[i] = ws[CHUNK:] + _dot(chains[i]["qk"], _bd_stack(vn))
        upds = [_dot(chains[i]["kdec_t"], vn.astype(BF16)) for i, vn in zip(idx, v_news)]
        states = [st * chains[i]["exp_last"] + jnp.where(state_mask, upd, 0.0)
                  for i, st, upd in zip(idx, states, upds)]
    for slot, st in enumerate(states):
        state_ref[slot] = st

    o_all = jnp.concatenate(
        [jnp.concatenate(outs[c * n_groups:(c + 1) * n_groups], axis=1) for c in range(bsz * chunks_per_batch)],
        axis=0)
    ms = head_sum(o_all * o_all) * (1.0 / HEAD_DIM)
    z = dl_ref[:, :, qkv_w:].reshape(tc, DELTA_WIDTH)
    o_ref[...] = (o_all * lax.rsqrt(ms + EPS) * ng_ref[...] * _silu(z)).reshape(bsz, rows_per_batch, DELTA_WIDTH)


def _delta_call(dl, ba, alog_row, dtb_row, ng_row):
    bsz, s, wd = dl.shape
    n_groups = DELTA_WIDTH // GROUP_W
    step_rows = DELTA_CHUNKS_PER_STEP * CHUNK
    chunk_spec = lambda w: pl.BlockSpec((bsz, step_rows, w), lambda i: (0, i, 0))
    return pl.pallas_call(
        _delta_kernel,
        grid=(s // step_rows,),
        in_specs=[
            chunk_spec(wd),
            chunk_spec(LANES),
            _const_spec((1, LANES)),
            _const_spec((1, LANES)),
            _const_spec((1, DELTA_WIDTH)),
        ],
        out_specs=chunk_spec(DELTA_WIDTH),
        out_shape=jax.ShapeDtypeStruct((bsz, s, DELTA_WIDTH), F32),
        scratch_shapes=[pltpu.VMEM((bsz * n_groups, GROUP_W, GROUP_W), F32)],
        compiler_params=_compiler_params(("arbitrary",)),
        name="delta",
    )(dl, ba, alog_row, dtb_row, ng_row)


def _out_ffn_kernel(x_ref, o1_ref, o2_ref, o3_ref, l1_ref, l2_ref, l3_ref, yd_ref, mod_ref,
                    wtop_ref, wbot_ref, g2_ref, wg_ref, wu_ref, wdn_ref, gf_ref, out_ref, *, final_norm):
    lses = [l1_ref[0], l2_ref[0], l3_ref[0]]
    outs = [o1_ref[0], o2_ref[0], o3_ref[0]]
    mx = jnp.maximum(jnp.maximum(lses[0], lses[1]), lses[2])
    ws = [jnp.exp(l - mx) for l in lses]
    den = ws[0] + ws[1] + ws[2]
    ya = (ws[0] * outs[0] + ws[1] * outs[1] + ws[2] * outs[2]) / den
    y = _dot(ya.astype(BF16), wtop_ref[...]) + _dot(yd_ref[0].astype(BF16), wbot_ref[...])
    x1 = x_ref[0] + mod_ref[0, 2:3, :] * y
    h = _rmsnorm_rows(x1, g2_ref[...]) * (1.0 + mod_ref[0, 4:5, :]) + mod_ref[0, 3:4, :]
    hb = h.astype(BF16)
    gate = _dot(hb, wg_ref[...])
    up = _dot(hb, wu_ref[...])
    y2 = _dot((_silu(gate) * up).astype(BF16), wdn_ref[...])
    x2 = x1 + mod_ref[0, 5:6, :] * y2
    out_ref[0] = _rmsnorm_rows(x2, gf_ref[...]) if final_norm else x2


def _out_ffn_call(x, o_list, lse_list, yd, mod3, wtop, wbot, g2, wg, wu, wdn, gf, final_norm):
    bsz, s, d = x.shape
    tm = FFN_TOKEN_TILE
    aw = o_list[0].shape[-1]
    dff = wg.shape[1]
    row_spec = lambda w: pl.BlockSpec((1, tm, w), lambda b, i: (b, i, 0))
    return pl.pallas_call(
        functools.partial(_out_ffn_kernel, final_norm=final_norm),
        grid=(bsz, s // tm),
        in_specs=[row_spec(d)] + [row_spec(aw)] * 6 + [row_spec(yd.shape[-1])] + [
            pl.BlockSpec((1, 6, d), lambda b, i: (b, 0, 0)),
            _const_spec(wtop.shape), _const_spec(wbot.shape), _const_spec((1, d)),
            _const_spec((d, dff)), _const_spec((d, dff)), _const_spec((dff, d)), _const_spec((1, d)),
        ],
        out_specs=row_spec(d),
        out_shape=jax.ShapeDtypeStruct((bsz, s, d), x.dtype),
        compiler_params=_compiler_params(("parallel", "parallel")),
        name="out_ffn",
    )(x, *o_list, *lse_list, yd, mod3, wtop, wbot, g2, wg, wu, wdn, gf)


def kernel(x, c, w_ada, b_ada, norm_attn_g, w_in, rel_bias, conv_w, a_log, dt_bias, delta_norm_g, w_out,
           norm_ffn_g, w_gate, w_up, w_down, final_norm_g):
    bsz, s, d = x.shape
    depth = w_ada.shape[0]
    assert s % PAD_UNIT == 0 and s % (DELTA_CHUNKS_PER_STEP * CHUNK) == 0 and s % TOKEN_TILE == 0
    assert all(window // dilation == BAND for window, dilation in DILATED_BRANCHES)

    bias = _bias_call(_bucket_index_tables(), rel_bias)
    lane_pad = LANES - 2 * N_DELTA_HEADS
    for l in range(depth):
        mod3 = _mod_call(c, w_ada[l], b_ada[l]).reshape(bsz, 6, d)
        qkv_w = 3 * ATTN_WIDTH
        dl_w = 4 * DELTA_WIDTH
        conv_cols = 3 * DELTA_WIDTH
        wa = w_in[l][:, :qkv_w].astype(BF16)
        wc = w_in[l][:, qkv_w:qkv_w + conv_cols].astype(BF16)
        wz = w_in[l][:, qkv_w + conv_cols:qkv_w + dl_w].astype(BF16)
        wba = jnp.pad(w_in[l][:, qkv_w + dl_w:], ((0, 0), (0, lane_pad))).astype(BF16)
        *qkv_by_branch, dl, ba = _inproj_call(
            x, mod3, norm_attn_g[l].reshape(1, d), wa, wc, wz, wba, conv_w[l].astype(F32))

        o_list, lse_list = [], []
        for branch, ((_, dilation), qkv) in enumerate(zip(DILATED_BRANCHES, qkv_by_branch)):
            o, lse = _attn_call(qkv.reshape(bsz, qkv.shape[1], s, LANES), bias, branch, dilation)
            o_list.append(o)
            lse_list.append(lse)

        gate_pad = lambda v: jnp.pad(v.astype(F32), (N_DELTA_HEADS, LANES - 2 * N_DELTA_HEADS)).reshape(1, LANES)
        yd = _delta_call(dl, ba, gate_pad(a_log[l]), gate_pad(dt_bias[l]),
                         jnp.tile(delta_norm_g[l].astype(F32), N_DELTA_HEADS).reshape(1, DELTA_WIDTH))

        wo = w_out[l].astype(BF16)
        x = _out_ffn_call(
            x, o_list, lse_list, yd, mod3, wo[:ATTN_WIDTH], wo[ATTN_WIDTH:], norm_ffn_g[l].reshape(1, d),
            w_gate[l].astype(BF16), w_up[l].astype(BF16), w_down[l].astype(BF16), final_norm_g.reshape(1, d),
            final_norm=(l == depth - 1))
    return x
```

```python
import functools
import math

import jax
import jax.numpy as jnp
import numpy as np
from jax import lax
from jax.experimental import pallas as pl
from jax.experimental.pallas import tpu as pltpu

F32 = jnp.float32
BF16 = jnp.bfloat16

HEAD_DIM = 64
N_ATTN_HEADS = 8
N_DELTA_HEADS = 8
ATTN_WIDTH = N_ATTN_HEADS * HEAD_DIM
DELTA_WIDTH = N_DELTA_HEADS * HEAD_DIM
DILATED_BRANCHES = ((128, 1), (512, 4), (2048, 16))
BAND = 128
D4, D16 = DILATED_BRANCHES[1][1], DILATED_BRANCHES[2][1]
PAD_UNIT = 2048
N_BUCKETS = 32
MAX_DISTANCE = 2048
CONV_WIDTH = 4
CHUNK = 64
EPS = 1e-6
NEG_INF = -1e30

LANES = 128
MXU_COLS = 256
VMEM_LIMIT_BYTES = 60000 * 1024

TOKEN_TILE = 512
HALO = 16
FFN_TOKEN_TILE = 512
ATTN_UNROLL = 8
HEADS_PER_GROUP = 4
GROUP_W = HEADS_PER_GROUP * HEAD_DIM


def _dot(a, b):
    return jnp.dot(a, b, preferred_element_type=F32)


def _dot_nt(a, b):
    return lax.dot_general(a, b, (((1,), (1,)), ((), ())), preferred_element_type=F32)


def _split3(x):
    x1 = x.astype(BF16)
    r1 = x - x1.astype(F32)
    x2 = r1.astype(BF16)
    x3 = (r1 - x2.astype(F32)).astype(BF16)
    return x1, x2, x3


def _split2(x):
    x1 = x.astype(BF16)
    x2 = (x - x1.astype(F32)).astype(BF16)
    return x1, x2


def _sigmoid(x):
    return 1.0 / (1.0 + jnp.exp(-x))


def _silu(x):
    return x * _sigmoid(x)


def _head_of(idx):
    return lax.shift_right_logical(idx, int(math.log2(HEAD_DIM)))


def _within_head(idx):
    return lax.bitwise_and(idx, HEAD_DIM - 1)


def _compiler_params(semantics):
    return pltpu.CompilerParams(dimension_semantics=semantics, vmem_limit_bytes=VMEM_LIMIT_BYTES)


def _const_spec(shape):
    zeros = (0,) * len(shape)
    return pl.BlockSpec(shape, lambda *_: zeros, pipeline_mode=pl.Buffered(1))


def _mod_kernel(c_ref, w_ref, b_ref, o_ref):
    ca = _silu(c_ref[...])
    acc = b_ref[...]
    w = w_ref[...]
    for part in _split3(ca):
        for wpart in _split3(w):
            acc = acc + _dot(part, wpart)
    o_ref[...] = acc


def _mod_call(c, w_ada, b_ada):
    bsz, d = c.shape
    n = w_ada.shape[1]
    blk = d
    return pl.pallas_call(
        _mod_kernel,
        grid=(n // blk,),
        in_specs=[
            pl.BlockSpec((bsz, d), lambda j: (0, 0)),
            pl.BlockSpec((d, blk), lambda j: (0, j)),
            pl.BlockSpec((1, blk), lambda j: (0, j)),
        ],
        out_specs=pl.BlockSpec((bsz, blk), lambda j: (0, j)),
        out_shape=jax.ShapeDtypeStruct((bsz, n), F32),
        compiler_params=_compiler_params(("arbitrary",)),
        name="mod",
    )(c, w_ada, b_ada.reshape(1, n))


def _rmsnorm_rows(x, g):
    return x * lax.rsqrt(jnp.mean(x * x, axis=-1, keepdims=True) + EPS) * g


def _inproj_kernel(x_ref, xh_ref, mod_ref, g_ref, wa_ref, wc_ref, wz_ref, wba_ref, cw_ref,
                   qkv1_ref, qkv4_ref, qkv16_ref, dl_ref, ba_ref, hb_ref, *block_scratch):
    tm = x_ref.shape[1]
    conv_w = wc_ref.shape[1]
    n_attn_blocks = wa_ref.shape[1] // MXU_COLS
    stage_refs, pre_refs = block_scratch[:2 * n_attn_blocks], block_scratch[2 * n_attn_blocks:]
    xe = jnp.concatenate([xh_ref[0], x_ref[0]], axis=0)
    h = _rmsnorm_rows(xe, g_ref[...]) * (1.0 + mod_ref[0, 1:2, :]) + mod_ref[0, 0:1, :]
    hb_ref[...] = h.astype(BF16)

    n_slabs = wa_ref.shape[1] // LANES
    for j0 in range(0, n_slabs, MXU_COLS // LANES):
        blk = _dot(hb_ref[HALO:, :], wa_ref[:, j0 * LANES:j0 * LANES + MXU_COLS])
        stage_ref, stage4_ref = stage_refs[2 * (j0 * LANES // MXU_COLS):][:2]
        for j in range(j0, j0 + MXU_COLS // LANES):
            jj = j - j0
            slab = blk[:, jj * LANES:(jj + 1) * LANES]
            stage_ref[jj] = slab
            qkv1_ref[0, j] = slab.astype(BF16)
            for r in range(D4):
                cls = stage_ref[jj, pl.ds(r, tm // D4, stride=D4), :]
                stage4_ref[jj, r] = cls
                qkv4_ref[0, j, r] = cls.astype(BF16)
            for r in range(D16):
                sub = stage4_ref[jj, r % D4, pl.ds(r // D4, tm // D16, stride=D16 // D4), :]
                qkv16_ref[0, j, r] = sub.astype(BF16)

    row = lax.broadcasted_iota(jnp.int32, (tm + HALO, MXU_COLS), 0)
    no_history = (row < HALO) & (pl.program_id(1) == 0)
    for c0 in range(0, conv_w, MXU_COLS):
        cols = slice(c0, c0 + MXU_COLS)
        pre = jnp.where(no_history, 0.0, _dot(hb_ref[...], wc_ref[:, cols]))
        pre_ref = pre_refs[c0 // MXU_COLS]
        pre_ref[...] = pre
        conv = pre[HALO:] * cw_ref[CONV_WIDTH - 1:CONV_WIDTH, cols]
        for back in range(1, CONV_WIDTH):
            tap = cw_ref[CONV_WIDTH - 1 - back:CONV_WIDTH - back, cols]
            conv = conv + pre_ref[HALO - back:HALO - back + tm, :] * tap
        dl_ref[0, :, cols] = _silu(conv)
    for c0 in range(0, wz_ref.shape[1], MXU_COLS):
        dl_ref[0, :, conv_w + c0:conv_w + c0 + MXU_COLS] = _dot(hb_ref[HALO:, :], wz_ref[:, c0:c0 + MXU_COLS])
    ba_ref[0] = _dot(hb_ref[HALO:, :], wba_ref[...])


def _inproj_call(x, mod3, g, wa, wc, wz, wba, conv_w):
    bsz, s, d = x.shape
    tm = TOKEN_TILE
    na, nc, nz, nba = wa.shape[1], wc.shape[1], wz.shape[1], wba.shape[1]
    n_slabs = na // LANES
    d4, d16 = DILATED_BRANCHES[1][1], DILATED_BRANCHES[2][1]
    row_spec = lambda w: pl.BlockSpec((1, tm, w), lambda b, i: (b, i, 0))
    cm_spec = lambda dil: pl.BlockSpec((1, n_slabs, dil, tm // dil, LANES), lambda b, i: (b, 0, 0, i, 0))
    cm_shape = lambda dil: jax.ShapeDtypeStruct((bsz, n_slabs, dil, s // dil, LANES), BF16)
    return pl.pallas_call(
        _inproj_kernel,
        grid=(bsz, s // tm),
        in_specs=[
            row_spec(d),
            pl.BlockSpec((1, HALO, d), lambda b, i: (b, jnp.maximum(i * (tm // HALO) - 1, 0), 0)),
            pl.BlockSpec((1, 6, d), lambda b, i: (b, 0, 0)),
            _const_spec((1, d)),
            _const_spec((d, na)),
            _const_spec((d, nc)),
            _const_spec((d, nz)),
            _const_spec((d, nba)),
            _const_spec(conv_w.shape),
        ],
        out_specs=[
            pl.BlockSpec((1, n_slabs, tm, LANES), lambda b, i: (b, 0, i, 0)),
            cm_spec(d4),
            cm_spec(d16),
            row_spec(nc + nz),
            row_spec(nba),
        ],
        out_shape=[
            jax.ShapeDtypeStruct((bsz, n_slabs, s, LANES), BF16),
            cm_shape(d4),
            cm_shape(d16),
            jax.ShapeDtypeStruct((bsz, s, nc + nz), F32),
            jax.ShapeDtypeStruct((bsz, s, nba), F32),
        ],
        scratch_shapes=(
            [pltpu.VMEM((tm + HALO, d), BF16)]
            + [pltpu.VMEM((MXU_COLS // LANES, tm, LANES), F32),
               pltpu.VMEM((MXU_COLS // LANES, D4, tm // D4, LANES), F32)] * (na // MXU_COLS)
            + [pltpu.VMEM((tm + HALO, MXU_COLS), F32)] * (nc // MXU_COLS)
        ),
        compiler_params=_compiler_params(("parallel", "parallel")),
        name="inproj",
    )(x, x, mod3, g, wa, wc, wz, wba, conv_w)


def _t5_bucket(distance):
    max_exact = N_BUCKETS // 2
    dist_f = jnp.maximum(distance, 1).astype(F32)
    large = max_exact + (jnp.log(dist_f / max_exact) / math.log(MAX_DISTANCE / max_exact)
                         * (N_BUCKETS - max_exact)).astype(jnp.int32)
    return jnp.where(distance < max_exact, distance, jnp.minimum(large, N_BUCKETS - 1))


def _bucket_index_tables():
    qi = jnp.arange(BAND)[:, None]
    kj = jnp.arange(2 * BAND)[None, :]
    steps = qi + BAND - kj
    in_window = (steps >= 0) & (steps <= BAND)
    tables = []
    for _, dilation in DILATED_BRANCHES:
        bucket = _t5_bucket(jnp.maximum(steps, 0) * dilation)
        v0 = jnp.where(in_window, bucket, -1)
        v1 = jnp.where(in_window & (kj >= BAND), bucket, -1)
        v2 = jnp.concatenate([v0[:, BAND:], jnp.full((BAND, BAND), -1, jnp.int32)], axis=1)
        tables.append(jnp.stack([v0, v1, v2]))
    return jnp.stack(tables).astype(jnp.int32)


def _bias_kernel(idx_ref, rb_ref, o_ref):
    idx = idx_ref[0]
    for h in range(N_ATTN_HEADS):
        acc = jnp.full(idx.shape, NEG_INF, F32)
        for b in range(N_BUCKETS):
            acc = jnp.where(idx == b, rb_ref[b, h], acc)
        o_ref[0, h] = acc


def _bias_call(idx_tables, rel_bias):
    nbr = idx_tables.shape[0]
    blk = idx_tables.shape[1:]
    return pl.pallas_call(
        _bias_kernel,
        grid=(nbr,),
        in_specs=[
            pl.BlockSpec((1,) + blk, lambda i: (i, 0, 0, 0)),
            pl.BlockSpec(memory_space=pltpu.SMEM),
        ],
        out_specs=pl.BlockSpec((1, N_ATTN_HEADS) + blk, lambda i: (i, 0, 0, 0, 0)),
        out_shape=jax.ShapeDtypeStruct((nbr, N_ATTN_HEADS) + blk, F32),
        compiler_params=_compiler_params(("arbitrary",)),
        name="bias",
    )(idx_tables, rel_bias.astype(F32))


def _attn_kernel(q_ref, k_ref, v_ref, bias_ref, o_ref, lse_ref, *, dilation, blocks_per_class, n_blocks):
    lane = lax.broadcasted_iota(jnp.int32, (BAND, LANES), 1)
    is_a = lane < HEAD_DIM
    class_shift = int(math.log2(blocks_per_class))

    def body(jg, carry):
        blocks = []
        for u in range(ATTN_UNROLL):
            j = jg * ATTN_UNROLL + u
            r0 = pl.multiple_of(j * BAND, BAND)
            ks = pl.multiple_of(jnp.maximum(j - 1, 0) * BAND, BAND)
            n_in_class = lax.bitwise_and(j, blocks_per_class - 1)
            variant = jnp.where(n_in_class == 0, jnp.where(j > 0, 1, 2), 0)
            q = q_ref[0, 0, pl.ds(r0, BAND), :] * (HEAD_DIM ** -0.5)
            blocks.append(dict(
                q=q, kw=k_ref[0, 0, pl.ds(ks, 2 * BAND), :], vw=v_ref[0, 0, pl.ds(ks, 2 * BAND), :], variant=variant,
                row0=lax.shift_right_logical(j, class_shift) + dilation * BAND * n_in_class))
        chains = [(blk, hh) for blk in blocks for hh in range(2)]
        zero = jnp.zeros((BAND, LANES), BF16)
        s_all = [_dot_nt(jnp.where(is_a if hh == 0 else jnp.logical_not(is_a), blk["q"], zero), blk["kw"])
                 + bias_ref[0, hh, blk["variant"]] for blk, hh in chains]
        m_all = [jnp.max(s, axis=-1, keepdims=True) for s in s_all]
        e_all = [jnp.exp(s - m) for s, m in zip(s_all, m_all)]
        d_all = [jnp.sum(e, axis=-1, keepdims=True) for e in e_all]
        o_all = [_dot(e.astype(BF16), blk["vw"]) / d for e, d, (blk, _) in zip(e_all, d_all, chains)]
        lse_all = [jnp.broadcast_to(m + jnp.log(d), (BAND, LANES)) for m, d in zip(m_all, d_all)]
        for u, blk in enumerate(blocks):
            rows = pl.ds(blk["row0"], BAND, stride=dilation)
            o_ref[0, rows, :] = jnp.where(is_a, o_all[2 * u], o_all[2 * u + 1])
            lse_ref[0, rows, :] = jnp.where(is_a, lse_all[2 * u], lse_all[2 * u + 1])
        return carry

    lax.fori_loop(0, n_blocks // ATTN_UNROLL, body, 0)


def _attn_call(qkv, bias, branch, dilation):
    bsz, n_slabs, s, _ = qkv.shape
    hp = n_slabs // 3
    n_blocks = s // BAND
    blocks_per_class = n_blocks // dilation
    assert blocks_per_class & (blocks_per_class - 1) == 0 and n_blocks % ATTN_UNROLL == 0
    kern = functools.partial(_attn_kernel, dilation=dilation, blocks_per_class=blocks_per_class, n_blocks=n_blocks)
    slab_spec = lambda t: pl.BlockSpec((1, 1, s, LANES), lambda h, b: (b, t * hp + h, 0, 0))
    out_spec = pl.BlockSpec((1, s, LANES), lambda h, b: (b, 0, h))
    return pl.pallas_call(
        kern,
        grid=(hp, bsz),
        in_specs=[
            slab_spec(0), slab_spec(1), slab_spec(2),
            pl.BlockSpec((1, 2, 3, BAND, 2 * BAND), lambda h, b: (branch, h, 0, 0, 0)),
        ],
        out_specs=[out_spec, out_spec],
        out_shape=[jax.ShapeDtypeStruct((bsz, s, hp * LANES), F32)] * 2,
        compiler_params=_compiler_params(("parallel", "parallel")),
        name=f"attn_d{dilation}",
    )(qkv, qkv, qkv, bias)


def _bd_stack(x):
    xb = x.astype(BF16)
    head_of_lane = _head_of(lax.broadcasted_iota(jnp.int32, xb.shape, 1))
    zero = jnp.zeros_like(xb)
    return jnp.concatenate([jnp.where(head_of_lane == h, xb, zero) for h in range(HEADS_PER_GROUP)], axis=0)


def _delta_kernel(dl_ref, ba_ref, alog_ref, dtb_ref, ng_ref, o_ref, state_ref):
    bsz = dl_ref.shape[0]
    tc = bsz * CHUNK
    qkv_w = 3 * DELTA_WIDTH
    n_groups = DELTA_WIDTH // GROUP_W

    @pl.when(pl.program_id(0) == 0)
    def _():
        state_ref[...] = jnp.zeros_like(state_ref)

    q_raw = dl_ref[:, :, :DELTA_WIDTH].reshape(tc, DELTA_WIDTH)
    k_raw = dl_ref[:, :, DELTA_WIDTH:2 * DELTA_WIDTH].reshape(tc, DELTA_WIDTH)
    v_all = dl_ref[:, :, 2 * DELTA_WIDTH:qkv_w].reshape(tc, DELTA_WIDTH)

    r512 = _head_of(lax.broadcasted_iota(jnp.int32, (DELTA_WIDTH, DELTA_WIDTH), 0))
    c512 = _head_of(lax.broadcasted_iota(jnp.int32, (DELTA_WIDTH, DELTA_WIDTH), 1))
    head_ones = (r512 == c512).astype(BF16)

    def head_sum(x):
        return _dot(x.astype(BF16), head_ones)

    q_all = q_raw * lax.rsqrt(head_sum(q_raw * q_raw) + EPS) * (HEAD_DIM ** -0.5)
    k_all = k_raw * lax.rsqrt(head_sum(k_raw * k_raw) + EPS)

    t = ba_ref[...].reshape(tc, LANES)
    za = t + dtb_ref[...]
    softplus = jnp.maximum(za, 0.0) + jnp.log1p(jnp.exp(-jnp.abs(za)))
    g_lane = -jnp.exp(alog_ref[...]) * softplus
    er = lax.broadcasted_iota(jnp.int32, (LANES, DELTA_WIDTH), 0)
    ec = _head_of(lax.broadcasted_iota(jnp.int32, (LANES, DELTA_WIDTH), 1))
    expand_beta = (er == ec).astype(BF16)
    expand_g = (er == ec + N_DELTA_HEADS).astype(BF16)
    beta_all = _dot(_sigmoid(t).astype(BF16), expand_beta)
    g_all = jnp.zeros((tc, DELTA_WIDTH), F32)
    for part in _split3(g_lane):
        g_all = g_all + _dot(part, expand_g)

    ci = lax.broadcasted_iota(jnp.int32, (CHUNK, CHUNK), 0)
    cj = lax.broadcasted_iota(jnp.int32, (CHUNK, CHUNK), 1)
    lower_incl = (ci >= cj).astype(BF16)
    all_ones = jnp.ones((CHUNK, CHUNK), BF16)
    row_w = lax.broadcasted_iota(jnp.int32, (CHUNK, DELTA_WIDTH), 0)
    col_w = _within_head(lax.broadcasted_iota(jnp.int32, (CHUNK, DELTA_WIDTH), 1))
    upper_incl_w = row_w <= col_w
    row_g = lax.broadcasted_iota(jnp.int32, (CHUNK, GROUP_W), 0)
    col_g = _within_head(lax.broadcasted_iota(jnp.int32, (CHUNK, GROUP_W), 1))
    causal_g = row_g >= col_g
    strict_g = row_g > col_g
    eye_g = (row_g == col_g).astype(F32)
    level_masks = []
    for n in range(int(math.log2(CHUNK))):
        shr = lambda v, k: lax.shift_right_logical(v, k)
        level_masks.append((shr(row_g, n + 1) == shr(col_g, n + 1)) & (shr(row_g, n) > shr(col_g, n)))
    sr = _head_of(lax.broadcasted_iota(jnp.int32, (GROUP_W, GROUP_W), 0))
    sc = _head_of(lax.broadcasted_iota(jnp.int32, (GROUP_W, GROUP_W), 1))
    state_mask = sr == sc

    chains = []
    for c in range(bsz):
        rows = slice(c * CHUNK, (c + 1) * CHUNK)
        gb = g_all[rows]
        gcum = jnp.zeros((CHUNK, DELTA_WIDTH), F32)
        gcum_t = jnp.zeros((CHUNK, DELTA_WIDTH), F32)
        for part in _split3(gb):
            gcum = gcum + _dot(lower_incl, part)
        for part in _split3(jnp.where(upper_incl_w, gb, 0.0)):
            gcum_t = gcum_t + _dot(all_ones, part)
        g_last = gcum[CHUNK - 1:CHUNK, :]
        exp_g = jnp.exp(gcum)
        exp_dec = jnp.exp(g_last - gcum)
        exp_last = jnp.exp(g_last)

        kc, qc, vc, bb = k_all[rows], q_all[rows], v_all[rows], beta_all[rows]
        kb = kc * bb
        vb = vc * bb
        kbe = kb * exp_g
        qe = qc * exp_g
        kdec = kc * exp_dec
        for g in range(n_groups):
            gl = slice(g * GROUP_W, (g + 1) * GROUP_W)
            diff = gcum[:, gl] - gcum_t[:, gl]
            decay = jnp.where(causal_g, jnp.exp(jnp.where(causal_g, diff, 0.0)), 0.0)
            aq = _dot_nt(jnp.concatenate([kb[:, gl], qc[:, gl]], axis=0).astype(BF16), _bd_stack(kc[:, gl]))
            chains.append(dict(
                slot=c * n_groups + g, a=jnp.where(strict_g, aq[:CHUNK] * decay, 0.0),
                qk=(aq[CHUNK:] * decay).astype(BF16), qe=qe[:, gl], kdec_t=kdec[:, gl].T.astype(BF16),
                exp_last=exp_last[:, gl],
                rhs=jnp.concatenate([_bd_stack(vb[:, gl]), _bd_stack(kbe[:, gl])], axis=1)))

    t_invs = [eye_g - jnp.where(level_masks[0], ch["a"], 0.0) for ch in chains]
    for mask in level_masks[1:]:
        ys = [_dot(jnp.where(mask, ch["a"], 0.0).astype(BF16), _bd_stack(t)) for ch, t in zip(chains, t_invs)]
        t_invs = [t - _dot(t.astype(BF16), _bd_stack(y)) for t, y in zip(t_invs, ys)]
    uws = [_dot(t.astype(BF16), ch["rhs"]) for ch, t in zip(chains, t_invs)]

    states = [state_ref[ch["slot"]] for ch in chains]
    wss = [_dot(jnp.concatenate([uw[:, GROUP_W:], ch["qe"]], axis=0).astype(BF16), st.astype(BF16))
           for ch, uw, st in zip(chains, uws, states)]
    v_news = [uw[:, :GROUP_W] - ws[:CHUNK] for uw, ws in zip(uws, wss)]
    outs = [ws[CHUNK:] + _dot(ch["qk"], _bd_stack(vn)) for ch, ws, vn in zip(chains, wss, v_news)]
    upds = [_dot(ch["kdec_t"], vn.astype(BF16)) for ch, vn in zip(chains, v_news)]
    for ch, st, upd in zip(chains, states, upds):
        state_ref[ch["slot"]] = st * ch["exp_last"] + jnp.where(state_mask, upd, 0.0)

    o_all = jnp.concatenate(
        [jnp.concatenate(outs[c * n_groups:(c + 1) * n_groups], axis=1) for c in range(bsz)], axis=0)
    ms = head_sum(o_all * o_all) * (1.0 / HEAD_DIM)
    z = dl_ref[:, :, qkv_w:].reshape(tc, DELTA_WIDTH)
    o_ref[...] = (o_all * lax.rsqrt(ms + EPS) * ng_ref[...] * _silu(z)).reshape(bsz, CHUNK, DELTA_WIDTH)


def _delta_call(dl, ba, alog_row, dtb_row, ng_row):
    bsz, s, wd = dl.shape
    n_groups = DELTA_WIDTH // GROUP_W
    chunk_spec = lambda w: pl.BlockSpec((bsz, CHUNK, w), lambda i: (0, i, 0))
    return pl.pallas_call(
        _delta_kernel,
        grid=(s // CHUNK,),
        in_specs=[
            chunk_spec(wd),
            chunk_spec(LANES),
            _const_spec((1, LANES)),
            _const_spec((1, LANES)),
            _const_spec((1, DELTA_WIDTH)),
        ],
        out_specs=chunk_spec(DELTA_WIDTH),
        out_shape=jax.ShapeDtypeStruct((bsz, s, DELTA_WIDTH), F32),
        scratch_shapes=[pltpu.VMEM((bsz * n_groups, GROUP_W, GROUP_W), F32)],
        compiler_params=_compiler_params(("arbitrary",)),
        name="delta",
    )(dl, ba, alog_row, dtb_row, ng_row)


def _out_ffn_kernel(x_ref, o1_ref, o2_ref, o3_ref, l1_ref, l2_ref, l3_ref, yd_ref, mod_ref,
                    wtop_ref, wbot_ref, g2_ref, wg_ref, wu_ref, wdn_ref, gf_ref, out_ref, *, final_norm):
    lses = [l1_ref[0], l2_ref[0], l3_ref[0]]
    outs = [o1_ref[0], o2_ref[0], o3_ref[0]]
    mx = jnp.maximum(jnp.maximum(lses[0], lses[1]), lses[2])
    ws = [jnp.exp(l - mx) for l in lses]
    den = ws[0] + ws[1] + ws[2]
    ya = (ws[0] * outs[0] + ws[1] * outs[1] + ws[2] * outs[2]) / den
    y = _dot(ya.astype(BF16), wtop_ref[...]) + _dot(yd_ref[0].astype(BF16), wbot_ref[...])
    x1 = x_ref[0] + mod_ref[0, 2:3, :] * y
    h = _rmsnorm_rows(x1, g2_ref[...]) * (1.0 + mod_ref[0, 4:5, :]) + mod_ref[0, 3:4, :]
    hb = h.astype(BF16)
    gate = _dot(hb, wg_ref[...])
    up = _dot(hb, wu_ref[...])
    y2 = _dot((_silu(gate) * up).astype(BF16), wdn_ref[...])
    x2 = x1 + mod_ref[0, 5:6, :] * y2
    out_ref[0] = _rmsnorm_rows(x2, gf_ref[...]) if final_norm else x2


def _out_ffn_call(x, o_list, lse_list, yd, mod3, wtop, wbot, g2, wg, wu, wdn, gf, final_norm):
    bsz, s, d = x.shape
    tm = FFN_TOKEN_TILE
    aw = o_list[0].shape[-1]
    dff = wg.shape[1]
    row_spec = lambda w: pl.BlockSpec((1, tm, w), lambda b, i: (b, i, 0))
    return pl.pallas_call(
        functools.partial(_out_ffn_kernel, final_norm=final_norm),
        grid=(bsz, s // tm),
        in_specs=[row_spec(d)] + [row_spec(aw)] * 6 + [row_spec(yd.shape[-1])] + [
            pl.BlockSpec((1, 6, d), lambda b, i: (b, 0, 0)),
            _const_spec(wtop.shape), _const_spec(wbot.shape), _const_spec((1, d)),
            _const_spec((d, dff)), _const_spec((d, dff)), _const_spec((dff, d)), _const_spec((1, d)),
        ],
        out_specs=row_spec(d),
        out_shape=jax.ShapeDtypeStruct((bsz, s, d), x.dtype),
        compiler_params=_compiler_params(("parallel", "parallel")),
        name="out_ffn",
    )(x, *o_list, *lse_list, yd, mod3, wtop, wbot, g2, wg, wu, wdn, gf)


def kernel(x, c, w_ada, b_ada, norm_attn_g, w_in, rel_bias, conv_w, a_log, dt_bias, delta_norm_g, w_out,
           norm_ffn_g, w_gate, w_up, w_down, final_norm_g):
    bsz, s, d = x.shape
    depth = w_ada.shape[0]
    assert s % PAD_UNIT == 0 and s % CHUNK == 0 and s % TOKEN_TILE == 0
    assert all(window // dilation == BAND for window, dilation in DILATED_BRANCHES)

    bias = _bias_call(_bucket_index_tables(), rel_bias)
    lane_pad = LANES - 2 * N_DELTA_HEADS
    for l in range(depth):
        mod3 = _mod_call(c, w_ada[l], b_ada[l]).reshape(bsz, 6, d)
        qkv_w = 3 * ATTN_WIDTH
        dl_w = 4 * DELTA_WIDTH
        conv_cols = 3 * DELTA_WIDTH
        wa = w_in[l][:, :qkv_w].astype(BF16)
        wc = w_in[l][:, qkv_w:qkv_w + conv_cols].astype(BF16)
        wz = w_in[l][:, qkv_w + conv_cols:qkv_w + dl_w].astype(BF16)
        wba = jnp.pad(w_in[l][:, qkv_w + dl_w:], ((0, 0), (0, lane_pad))).astype(BF16)
        *qkv_by_branch, dl, ba = _inproj_call(
            x, mod3, norm_attn_g[l].reshape(1, d), wa, wc, wz, wba, conv_w[l].astype(F32))

        o_list, lse_list = [], []
        for branch, ((_, dilation), qkv) in enumerate(zip(DILATED_BRANCHES, qkv_by_branch)):
            o, lse = _attn_call(qkv.reshape(bsz, qkv.shape[1], s, LANES), bias, branch, dilation)
            o_list.append(o)
            lse_list.append(lse)

        gate_pad = lambda v: jnp.pad(v.astype(F32), (N_DELTA_HEADS, LANES - 2 * N_DELTA_HEADS)).reshape(1, LANES)
        yd = _delta_call(dl, ba, gate_pad(a_log[l]), gate_pad(dt_bias[l]),
                         jnp.tile(delta_norm_g[l].astype(F32), N_DELTA_HEADS).reshape(1, DELTA_WIDTH))

        wo = w_out[l].astype(BF16)
        x = _out_ffn_call(
            x, o_list, lse_list, yd, mod3, wo[:ATTN_WIDTH], wo[ATTN_WIDTH:], norm_ffn_g[l].reshape(1, d),
            w_gate[l].astype(BF16), w_up[l].astype(BF16), w_down[l].astype(BF16), final_norm_g.reshape(1, d),
            final_norm=(l == depth - 1))
    return x
```

```python
import functools
import math

import jax
import jax.numpy as jnp
import numpy as np
from jax import lax
from jax.experimental import pallas as pl
from jax.experimental.pallas import tpu as pltpu

F32 = jnp.float32
BF16 = jnp.bfloat16

HEAD_DIM = 64
N_ATTN_HEADS = 8
N_DELTA_HEADS = 8
ATTN_WIDTH = N_ATTN_HEADS * HEAD_DIM
DELTA_WIDTH = N_DELTA_HEADS * HEAD_DIM
DILATED_BRANCHES = ((128, 1), (512, 4), (2048, 16))
BAND = 128
D4, D16 = DILATED_BRANCHES[1][1], DILATED_BRANCHES[2][1]
PAD_UNIT = 2048
N_BUCKETS = 32
MAX_DISTANCE = 2048
CONV_WIDTH = 4
CHUNK = 64
EPS = 1e-6
NEG_INF = -1e30

LANES = 128
MXU_COLS = 256
VMEM_LIMIT_BYTES = 60000 * 1024

TOKEN_TILE = 512
HALO = 16
FFN_TOKEN_TILE = 512
FFN_ROW_GROUPS = 2
ATTN_UNROLL = 8
HEADS_PER_GROUP = 4
GROUP_W = HEADS_PER_GROUP * HEAD_DIM


def _dot(a, b):
    return jnp.dot(a, b, preferred_element_type=F32)


def _dot_nt(a, b):
    return lax.dot_general(a, b, (((1,), (1,)), ((), ())), preferred_element_type=F32)


def _split3(x):
    x1 = x.astype(BF16)
    r1 = x - x1.astype(F32)
    x2 = r1.astype(BF16)
    x3 = (r1 - x2.astype(F32)).astype(BF16)
    return x1, x2, x3


def _split2(x):
    x1 = x.astype(BF16)
    x2 = (x - x1.astype(F32)).astype(BF16)
    return x1, x2


def _sigmoid(x):
    return 1.0 / (1.0 + jnp.exp(-x))


def _silu(x):
    return x * _sigmoid(x)


def _head_of(idx):
    return lax.shift_right_logical(idx, int(math.log2(HEAD_DIM)))


def _within_head(idx):
    return lax.bitwise_and(idx, HEAD_DIM - 1)


def _compiler_params(semantics):
    return pltpu.CompilerParams(dimension_semantics=semantics, vmem_limit_bytes=VMEM_LIMIT_BYTES)


def _const_spec(shape):
    zeros = (0,) * len(shape)
    return pl.BlockSpec(shape, lambda *_: zeros, pipeline_mode=pl.Buffered(1))


def _mod_kernel(c_ref, w_ref, b_ref, o_ref):
    ca = _silu(c_ref[...])
    acc = b_ref[...]
    w = w_ref[...]
    for part in _split3(ca):
        for wpart in _split3(w):
            acc = acc + _dot(part, wpart)
    o_ref[...] = acc


def _mod_call(c, w_ada, b_ada):
    bsz, d = c.shape
    n = w_ada.shape[1]
    blk = d
    return pl.pallas_call(
        _mod_kernel,
        grid=(n // blk,),
        in_specs=[
            pl.BlockSpec((bsz, d), lambda j: (0, 0)),
            pl.BlockSpec((d, blk), lambda j: (0, j)),
            pl.BlockSpec((1, blk), lambda j: (0, j)),
        ],
        out_specs=pl.BlockSpec((bsz, blk), lambda j: (0, j)),
        out_shape=jax.ShapeDtypeStruct((bsz, n), F32),
        compiler_params=_compiler_params(("arbitrary",)),
        name="mod",
    )(c, w_ada, b_ada.reshape(1, n))


def _rmsnorm_rows(x, g):
    return x * lax.rsqrt(jnp.mean(x * x, axis=-1, keepdims=True) + EPS) * g


def _inproj_kernel(x_ref, xh_ref, mod_ref, g_ref, wa_ref, wc_ref, wz_ref, wba_ref, cw_ref,
                   qkv1_ref, qkv4_ref, qkv16_ref, dl_ref, ba_ref, hb_ref, *block_scratch):
    tm = x_ref.shape[1]
    conv_w = wc_ref.shape[1]
    n_attn_blocks = wa_ref.shape[1] // MXU_COLS
    stage_refs, pre_refs = block_scratch[:2 * n_attn_blocks], block_scratch[2 * n_attn_blocks:]
    xe = jnp.concatenate([xh_ref[0], x_ref[0]], axis=0)
    h = _rmsnorm_rows(xe, g_ref[...]) * (1.0 + mod_ref[0, 1:2, :]) + mod_ref[0, 0:1, :]
    hb_ref[...] = h.astype(BF16)

    n_slabs = wa_ref.shape[1] // LANES

    def attn_block(j0):
        blk = _dot(hb_ref[HALO:, :], wa_ref[:, j0 * LANES:j0 * LANES + MXU_COLS])
        stage_ref, stage4_ref = stage_refs[2 * (j0 * LANES // MXU_COLS):][:2]
        for j in range(j0, j0 + MXU_COLS // LANES):
            jj = j - j0
            slab = blk[:, jj * LANES:(jj + 1) * LANES]
            stage_ref[jj] = slab
            qkv1_ref[0, j] = slab.astype(BF16)
            for r in range(D4):
                cls = stage_ref[jj, pl.ds(r, tm // D4, stride=D4), :]
                stage4_ref[jj, r] = cls
                qkv4_ref[0, j, r] = cls.astype(BF16)
            for r in range(D16):
                sub = stage4_ref[jj, r % D4, pl.ds(r // D4, tm // D16, stride=D16 // D4), :]
                qkv16_ref[0, j, r] = sub.astype(BF16)

    row = lax.broadcasted_iota(jnp.int32, (tm + HALO, MXU_COLS), 0)
    no_history = (row < HALO) & (pl.program_id(1) == 0)

    def conv_block(c0):
        cols = slice(c0, c0 + MXU_COLS)
        pre = jnp.where(no_history, 0.0, _dot(hb_ref[...], wc_ref[:, cols]))
        pre_ref = pre_refs[c0 // MXU_COLS]
        pre_ref[...] = pre
        conv = pre[HALO:] * cw_ref[CONV_WIDTH - 1:CONV_WIDTH, cols]
        for back in range(1, CONV_WIDTH):
            tap = cw_ref[CONV_WIDTH - 1 - back:CONV_WIDTH - back, cols]
            conv = conv + pre_ref[HALO - back:HALO - back + tm, :] * tap
        dl_ref[0, :, cols] = _silu(conv)

    attn_starts = list(range(0, n_slabs, MXU_COLS // LANES))
    conv_starts = list(range(0, conv_w, MXU_COLS))
    while attn_starts or conv_starts:
        if conv_starts:
            conv_block(conv_starts.pop(0))
        if attn_starts:
            attn_block(attn_starts.pop(0))
    for c0 in range(0, wz_ref.shape[1], MXU_COLS):
        dl_ref[0, :, conv_w + c0:conv_w + c0 + MXU_COLS] = _dot(hb_ref[HALO:, :], wz_ref[:, c0:c0 + MXU_COLS])
    ba_ref[0] = _dot(hb_ref[HALO:, :], wba_ref[...])


def _inproj_call(x, mod3, g, wa, wc, wz, wba, conv_w):
    bsz, s, d = x.shape
    tm = TOKEN_TILE
    na, nc, nz, nba = wa.shape[1], wc.shape[1], wz.shape[1], wba.shape[1]
    n_slabs = na // LANES
    d4, d16 = DILATED_BRANCHES[1][1], DILATED_BRANCHES[2][1]
    row_spec = lambda w: pl.BlockSpec((1, tm, w), lambda b, i: (b, i, 0))
    cm_spec = lambda dil: pl.BlockSpec((1, n_slabs, dil, tm // dil, LANES), lambda b, i: (b, 0, 0, i, 0))
    cm_shape = lambda dil: jax.ShapeDtypeStruct((bsz, n_slabs, dil, s // dil, LANES), BF16)
    return pl.pallas_call(
        _inproj_kernel,
        grid=(bsz, s // tm),
        in_specs=[
            row_spec(d),
            pl.BlockSpec((1, HALO, d), lambda b, i: (b, jnp.maximum(i * (tm // HALO) - 1, 0), 0)),
            pl.BlockSpec((1, 6, d), lambda b, i: (b, 0, 0)),
            _const_spec((1, d)),
            _const_spec((d, na)),
            _const_spec((d, nc)),
            _const_spec((d, nz)),
            _const_spec((d, nba)),
            _const_spec(conv_w.shape),
        ],
        out_specs=[
            pl.BlockSpec((1, n_slabs, tm, LANES), lambda b, i: (b, 0, i, 0)),
            cm_spec(d4),
            cm_spec(d16),
            row_spec(nc + nz),
            row_spec(nba),
        ],
        out_shape=[
            jax.ShapeDtypeStruct((bsz, n_slabs, s, LANES), BF16),
            cm_shape(d4),
            cm_shape(d16),
            jax.ShapeDtypeStruct((bsz, s, nc + nz), F32),
            jax.ShapeDtypeStruct((bsz, s, nba), F32),
        ],
        scratch_shapes=(
            [pltpu.VMEM((tm + HALO, d), BF16)]
            + [pltpu.VMEM((MXU_COLS // LANES, tm, LANES), F32),
               pltpu.VMEM((MXU_COLS // LANES, D4, tm // D4, LANES), F32)] * (na // MXU_COLS)
            + [pltpu.VMEM((tm + HALO, MXU_COLS), F32)] * (nc // MXU_COLS)
        ),
        compiler_params=_compiler_params(("parallel", "parallel")),
        name="inproj",
    )(x, x, mod3, g, wa, wc, wz, wba, conv_w)


def _t5_bucket(distance):
    max_exact = N_BUCKETS // 2
    dist_f = jnp.maximum(distance, 1).astype(F32)
    large = max_exact + (jnp.log(dist_f / max_exact) / math.log(MAX_DISTANCE / max_exact)
                         * (N_BUCKETS - max_exact)).astype(jnp.int32)
    return jnp.where(distance < max_exact, distance, jnp.minimum(large, N_BUCKETS - 1))


def _bucket_index_tables():
    qi = jnp.arange(BAND)[:, None]
    kj = jnp.arange(2 * BAND)[None, :]
    steps = qi + BAND - kj
    in_window = (steps >= 0) & (steps <= BAND)
    tables = []
    for _, dilation in DILATED_BRANCHES:
        bucket = _t5_bucket(jnp.maximum(steps, 0) * dilation)
        v0 = jnp.where(in_window, bucket, -1)
        v1 = jnp.where(in_window & (kj >= BAND), bucket, -1)
        v2 = jnp.concatenate([v0[:, BAND:], jnp.full((BAND, BAND), -1, jnp.int32)], axis=1)
        tables.append(jnp.stack([v0, v1, v2]))
    return jnp.stack(tables).astype(jnp.int32)


def _bias_kernel(idx_ref, rb_ref, o_ref):
    idx = idx_ref[0]
    for h in range(N_ATTN_HEADS):
        acc = jnp.full(idx.shape, NEG_INF, F32)
        for b in range(N_BUCKETS):
            acc = jnp.where(idx == b, rb_ref[b, h], acc)
        o_ref[0, h] = acc


def _bias_call(idx_tables, rel_bias):
    nbr = idx_tables.shape[0]
    blk = idx_tables.shape[1:]
    return pl.pallas_call(
        _bias_kernel,
        grid=(nbr,),
        in_specs=[
            pl.BlockSpec((1,) + blk, lambda i: (i, 0, 0, 0)),
            pl.BlockSpec(memory_space=pltpu.SMEM),
        ],
        out_specs=pl.BlockSpec((1, N_ATTN_HEADS) + blk, lambda i: (i, 0, 0, 0, 0)),
        out_shape=jax.ShapeDtypeStruct((nbr, N_ATTN_HEADS) + blk, F32),
        compiler_params=_compiler_params(("arbitrary",)),
        name="bias",
    )(idx_tables, rel_bias.astype(F32))


def _attn_kernel(q_ref, k_ref, v_ref, bias_ref, o_ref, lse_ref, *, dilation, blocks_per_class, n_blocks):
    lane = lax.broadcasted_iota(jnp.int32, (BAND, LANES), 1)
    is_a = lane < HEAD_DIM
    class_shift = int(math.log2(blocks_per_class))

    def body(jg, carry):
        blocks = []
        for u in range(ATTN_UNROLL):
            j = jg * ATTN_UNROLL + u
            r0 = pl.multiple_of(j * BAND, BAND)
            ks = pl.multiple_of(jnp.maximum(j - 1, 0) * BAND, BAND)
            n_in_class = lax.bitwise_and(j, blocks_per_class - 1)
            variant = jnp.where(n_in_class == 0, jnp.where(j > 0, 1, 2), 0)
            q = q_ref[0, 0, pl.ds(r0, BAND), :] * (HEAD_DIM ** -0.5)
            blocks.append(dict(
                q=q, kw=k_ref[0, 0, pl.ds(ks, 2 * BAND), :], vw=v_ref[0, 0, pl.ds(ks, 2 * BAND), :], variant=variant,
                row0=lax.shift_right_logical(j, class_shift) + dilation * BAND * n_in_class))
        chains = [(blk, hh) for blk in blocks for hh in range(2)]
        zero = jnp.zeros((BAND, LANES), BF16)
        s_all = [_dot_nt(jnp.where(is_a if hh == 0 else jnp.logical_not(is_a), blk["q"], zero), blk["kw"])
                 + bias_ref[0, hh, blk["variant"]] for blk, hh in chains]
        m_all = [jnp.max(s, axis=-1, keepdims=True) for s in s_all]
        e_all = [jnp.exp(s - m) for s, m in zip(s_all, m_all)]
        d_all = [jnp.sum(e, axis=-1, keepdims=True) for e in e_all]
        o_all = [_dot(e.astype(BF16), blk["vw"]) / d for e, d, (blk, _) in zip(e_all, d_all, chains)]
        lse_all = [jnp.broadcast_to(m + jnp.log(d), (BAND, LANES)) for m, d in zip(m_all, d_all)]
        for u, blk in enumerate(blocks):
            rows = pl.ds(blk["row0"], BAND, stride=dilation)
            o_ref[0, rows, :] = jnp.where(is_a, o_all[2 * u], o_all[2 * u + 1])
            lse_ref[0, rows, :] = jnp.where(is_a, lse_all[2 * u], lse_all[2 * u + 1])
        return carry

    lax.fori_loop(0, n_blocks // ATTN_UNROLL, body, 0)


def _attn_call(qkv, bias, branch, dilation):
    bsz, n_slabs, s, _ = qkv.shape
    hp = n_slabs // 3
    n_blocks = s // BAND
    blocks_per_class = n_blocks // dilation
    assert blocks_per_class & (blocks_per_class - 1) == 0 and n_blocks % ATTN_UNROLL == 0
    kern = functools.partial(_attn_kernel, dilation=dilation, blocks_per_class=blocks_per_class, n_blocks=n_blocks)
    slab_spec = lambda t: pl.BlockSpec((1, 1, s, LANES), lambda h, b: (b, t * hp + h, 0, 0))
    out_spec = pl.BlockSpec((1, s, LANES), lambda h, b: (b, 0, h))
    return pl.pallas_call(
        kern,
        grid=(hp, bsz),
        in_specs=[
            slab_spec(0), slab_spec(1), slab_spec(2),
            pl.BlockSpec((1, 2, 3, BAND, 2 * BAND), lambda h, b: (branch, h, 0, 0, 0)),
        ],
        out_specs=[out_spec, out_spec],
        out_shape=[jax.ShapeDtypeStruct((bsz, s, hp * LANES), F32)] * 2,
        compiler_params=_compiler_params(("parallel", "parallel")),
        name=f"attn_d{dilation}",
    )(qkv, qkv, qkv, bias)


def _bd_stack(x):
    xb = x.astype(BF16)
    head_of_lane = _head_of(lax.broadcasted_iota(jnp.int32, xb.shape, 1))
    zero = jnp.zeros_like(xb)
    return jnp.concatenate([jnp.where(head_of_lane == h, xb, zero) for h in range(HEADS_PER_GROUP)], axis=0)


def _delta_kernel(dl_ref, ba_ref, alog_ref, dtb_ref, ng_ref, o_ref, state_ref):
    bsz = dl_ref.shape[0]
    tc = bsz * CHUNK
    qkv_w = 3 * DELTA_WIDTH
    n_groups = DELTA_WIDTH // GROUP_W

    @pl.when(pl.program_id(0) == 0)
    def _():
        state_ref[...] = jnp.zeros_like(state_ref)

    q_raw = dl_ref[:, :, :DELTA_WIDTH].reshape(tc, DELTA_WIDTH)
    k_raw = dl_ref[:, :, DELTA_WIDTH:2 * DELTA_WIDTH].reshape(tc, DELTA_WIDTH)
    v_all = dl_ref[:, :, 2 * DELTA_WIDTH:qkv_w].reshape(tc, DELTA_WIDTH)

    r512 = _head_of(lax.broadcasted_iota(jnp.int32, (DELTA_WIDTH, DELTA_WIDTH), 0))
    c512 = _head_of(lax.broadcasted_iota(jnp.int32, (DELTA_WIDTH, DELTA_WIDTH), 1))
    head_ones = (r512 == c512).astype(BF16)

    def head_sum(x):
        return _dot(x.astype(BF16), head_ones)

    q_all = q_raw * lax.rsqrt(head_sum(q_raw * q_raw) + EPS) * (HEAD_DIM ** -0.5)
    k_all = k_raw * lax.rsqrt(head_sum(k_raw * k_raw) + EPS)

    t = ba_ref[...].reshape(tc, LANES)
    za = t + dtb_ref[...]
    softplus = jnp.maximum(za, 0.0) + jnp.log1p(jnp.exp(-jnp.abs(za)))
    g_lane = -jnp.exp(alog_ref[...]) * softplus
    er = lax.broadcasted_iota(jnp.int32, (LANES, DELTA_WIDTH), 0)
    ec = _head_of(lax.broadcasted_iota(jnp.int32, (LANES, DELTA_WIDTH), 1))
    expand_beta = (er == ec).astype(BF16)
    expand_g = (er == ec + N_DELTA_HEADS).astype(BF16)
    beta_all = _dot(_sigmoid(t).astype(BF16), expand_beta)
    g_all = jnp.zeros((tc, DELTA_WIDTH), F32)
    for part in _split3(g_lane):
        g_all = g_all + _dot(part, expand_g)

    ci = lax.broadcasted_iota(jnp.int32, (CHUNK, CHUNK), 0)
    cj = lax.broadcasted_iota(jnp.int32, (CHUNK, CHUNK), 1)
    lower_incl = (ci >= cj).astype(BF16)
    all_ones = jnp.ones((CHUNK, CHUNK), BF16)
    row_w = lax.broadcasted_iota(jnp.int32, (CHUNK, DELTA_WIDTH), 0)
    col_w = _within_head(lax.broadcasted_iota(jnp.int32, (CHUNK, DELTA_WIDTH), 1))
    upper_incl_w = row_w <= col_w
    row_g = lax.broadcasted_iota(jnp.int32, (CHUNK, GROUP_W), 0)
    col_g = _within_head(lax.broadcasted_iota(jnp.int32, (CHUNK, GROUP_W), 1))
    causal_g = row_g >= col_g
    strict_g = row_g > col_g
    eye_g = (row_g == col_g).astype(F32)
    level_masks = []
    for n in range(int(math.log2(CHUNK))):
        shr = lambda v, k: lax.shift_right_logical(v, k)
        level_masks.append((shr(row_g, n + 1) == shr(col_g, n + 1)) & (shr(row_g, n) > shr(col_g, n)))
    sr = _head_of(lax.broadcasted_iota(jnp.int32, (GROUP_W, GROUP_W), 0))
    sc = _head_of(lax.broadcasted_iota(jnp.int32, (GROUP_W, GROUP_W), 1))
    state_mask = sr == sc

    chains = []
    for c in range(bsz):
        rows = slice(c * CHUNK, (c + 1) * CHUNK)
        gb = g_all[rows]
        gcum = jnp.zeros((CHUNK, DELTA_WIDTH), F32)
        gcum_t = jnp.zeros((CHUNK, DELTA_WIDTH), F32)
        for part in _split3(gb):
            gcum = gcum + _dot(lower_incl, part)
            gcum_t = gcum_t + _dot(all_ones, jnp.where(upper_incl_w, part, jnp.zeros_like(part)))
        g_last = gcum[CHUNK - 1:CHUNK, :]
        exp_g = jnp.exp(gcum)
        exp_dec = jnp.exp(g_last - gcum)
        exp_last = jnp.exp(g_last)

        kc, qc, vc, bb = k_all[rows], q_all[rows], v_all[rows], beta_all[rows]
        kb = kc * bb
        vb = vc * bb
        kbe = kb * exp_g
        qe = qc * exp_g
        kdec = kc * exp_dec
        for g in range(n_groups):
            gl = slice(g * GROUP_W, (g + 1) * GROUP_W)
            diff = gcum[:, gl] - gcum_t[:, gl]
            decay = jnp.where(causal_g, jnp.exp(jnp.where(causal_g, diff, 0.0)), 0.0)
            aq = _dot_nt(jnp.concatenate([kb[:, gl], qc[:, gl]], axis=0).astype(BF16), _bd_stack(kc[:, gl]))
            chains.append(dict(
                slot=c * n_groups + g, a=jnp.where(strict_g, aq[:CHUNK] * decay, 0.0),
                qk=(aq[CHUNK:] * decay).astype(BF16), qe=qe[:, gl], kdec_t=kdec[:, gl].T.astype(BF16),
                exp_last=exp_last[:, gl], vb=vb[:, gl].astype(BF16), kbe=kbe[:, gl].astype(BF16)))

    t_invs = [eye_g - jnp.where(level_masks[0], ch["a"], 0.0) for ch in chains]
    for mask in level_masks[1:]:
        ys = [_dot(jnp.where(mask, ch["a"], 0.0).astype(BF16), _bd_stack(t)) for ch, t in zip(chains, t_invs)]
        t_invs = [t - _dot(t.astype(BF16), _bd_stack(y)) for t, y in zip(t_invs, ys)]
    uws = [_dot(t.astype(BF16), jnp.concatenate([_bd_stack(ch["vb"]), _bd_stack(ch["kbe"])], axis=1))
           for ch, t in zip(chains, t_invs)]

    states = [state_ref[ch["slot"]] for ch in chains]
    wss = [_dot(jnp.concatenate([uw[:, GROUP_W:], ch["qe"]], axis=0).astype(BF16), st.astype(BF16))
           for ch, uw, st in zip(chains, uws, states)]
    v_news = [uw[:, :GROUP_W] - ws[:CHUNK] for uw, ws in zip(uws, wss)]
    outs = [ws[CHUNK:] + _dot(ch["qk"], _bd_stack(vn)) for ch, ws, vn in zip(chains, wss, v_news)]
    upds = [_dot(ch["kdec_t"], vn.astype(BF16)) for ch, vn in zip(chains, v_news)]
    for ch, st, upd in zip(chains, states, upds):
        state_ref[ch["slot"]] = st * ch["exp_last"] + jnp.where(state_mask, upd, 0.0)

    o_all = jnp.concatenate(
        [jnp.concatenate(outs[c * n_groups:(c + 1) * n_groups], axis=1) for c in range(bsz)], axis=0)
    ms = head_sum(o_all * o_all) * (1.0 / HEAD_DIM)
    z = dl_ref[:, :, qkv_w:].reshape(tc, DELTA_WIDTH)
    o_ref[...] = (o_all * lax.rsqrt(ms + EPS) * ng_ref[...] * _silu(z)).reshape(bsz, CHUNK, DELTA_WIDTH)


def _delta_call(dl, ba, alog_row, dtb_row, ng_row):
    bsz, s, wd = dl.shape
    n_groups = DELTA_WIDTH // GROUP_W
    chunk_spec = lambda w: pl.BlockSpec((bsz, CHUNK, w), lambda i: (0, i, 0))
    return pl.pallas_call(
        _delta_kernel,
        grid=(s // CHUNK,),
        in_specs=[
            chunk_spec(wd),
            chunk_spec(LANES),
            _const_spec((1, LANES)),
            _const_spec((1, LANES)),
            _const_spec((1, DELTA_WIDTH)),
        ],
        out_specs=chunk_spec(DELTA_WIDTH),
        out_shape=jax.ShapeDtypeStruct((bsz, s, DELTA_WIDTH), F32),
        scratch_shapes=[pltpu.VMEM((bsz * n_groups, GROUP_W, GROUP_W), F32)],
        compiler_params=_compiler_params(("arbitrary",)),
        name="delta",
    )(dl, ba, alog_row, dtb_row, ng_row)


def _out_ffn_kernel(x_ref, o1_ref, o2_ref, o3_ref, l1_ref, l2_ref, l3_ref, yd_ref, mod_ref,
                    wtop_ref, wbot_ref, g2_ref, wg_ref, wu_ref, wdn_ref, gf_ref, out_ref, *, final_norm):
    tm = x_ref.shape[1]
    groups = [slice(r, r + tm // FFN_ROW_GROUPS) for r in range(0, tm, tm // FFN_ROW_GROUPS)]

    def merge(rows):
        lses = [l1_ref[0, rows, :], l2_ref[0, rows, :], l3_ref[0, rows, :]]
        outs = [o1_ref[0, rows, :], o2_ref[0, rows, :], o3_ref[0, rows, :]]
        mx = jnp.maximum(jnp.maximum(lses[0], lses[1]), lses[2])
        ws = [jnp.exp(l - mx) for l in lses]
        den = ws[0] + ws[1] + ws[2]
        return ((ws[0] * outs[0] + ws[1] * outs[1] + ws[2] * outs[2]) / den).astype(BF16)

    ya = [merge(rows) for rows in groups]
    y = [_dot(a, wtop_ref[...]) + _dot(yd_ref[0, rows, :].astype(BF16), wbot_ref[...]) for a, rows in zip(ya, groups)]
    x1 = [x_ref[0, rows, :] + mod_ref[0, 2:3, :] * yy for yy, rows in zip(y, groups)]
    hb = [(_rmsnorm_rows(xx, g2_ref[...]) * (1.0 + mod_ref[0, 4:5, :]) + mod_ref[0, 3:4, :]).astype(BF16) for xx in x1]
    gate = [_dot(h, wg_ref[...]) for h in hb]
    up = [_dot(h, wu_ref[...]) for h in hb]
    y2 = [_dot((_silu(g) * u).astype(BF16), wdn_ref[...]) for g, u in zip(gate, up)]
    for rows, xx, yy in zip(groups, x1, y2):
        x2 = xx + mod_ref[0, 5:6, :] * yy
        out_ref[0, rows, :] = _rmsnorm_rows(x2, gf_ref[...]) if final_norm else x2


def _out_ffn_call(x, o_list, lse_list, yd, mod3, wtop, wbot, g2, wg, wu, wdn, gf, final_norm):
    bsz, s, d = x.shape
    tm = FFN_TOKEN_TILE
    aw = o_list[0].shape[-1]
    dff = wg.shape[1]
    row_spec = lambda w: pl.BlockSpec((1, tm, w), lambda b, i: (b, i, 0))
    return pl.pallas_call(
        functools.partial(_out_ffn_kernel, final_norm=final_norm),
        grid=(bsz, s // tm),
        in_specs=[row_spec(d)] + [row_spec(aw)] * 6 + [row_spec(yd.shape[-1])] + [
            pl.BlockSpec((1, 6, d), lambda b, i: (b, 0, 0)),
            _const_spec(wtop.shape), _const_spec(wbot.shape), _const_spec((1, d)),
            _const_spec((d, dff)), _const_spec((d, dff)), _const_spec((dff, d)), _const_spec((1, d)),
        ],
        out_specs=row_spec(d),
        out_shape=jax.ShapeDtypeStruct((bsz, s, d), x.dtype),
        compiler_params=_compiler_params(("parallel", "parallel")),
        name="out_ffn",
    )(x, *o_list, *lse_list, yd, mod3, wtop, wbot, g2, wg, wu, wdn, gf)


def kernel(x, c, w_ada, b_ada, norm_attn_g, w_in, rel_bias, conv_w, a_log, dt_bias, delta_norm_g, w_out,
           norm_ffn_g, w_gate, w_up, w_down, final_norm_g):
    bsz, s, d = x.shape
    depth = w_ada.shape[0]
    assert s % PAD_UNIT == 0 and s % CHUNK == 0 and s % TOKEN_TILE == 0
    assert all(window // dilation == BAND for window, dilation in DILATED_BRANCHES)

    bias = _bias_call(_bucket_index_tables(), rel_bias)
    lane_pad = LANES - 2 * N_DELTA_HEADS
    for l in range(depth):
        mod3 = _mod_call(c, w_ada[l], b_ada[l]).reshape(bsz, 6, d)
        qkv_w = 3 * ATTN_WIDTH
        dl_w = 4 * DELTA_WIDTH
        conv_cols = 3 * DELTA_WIDTH
        wa = w_in[l][:, :qkv_w].astype(BF16)
        wc = w_in[l][:, qkv_w:qkv_w + conv_cols].astype(BF16)
        wz = w_in[l][:, qkv_w + conv_cols:qkv_w + dl_w].astype(BF16)
        wba = jnp.pad(w_in[l][:, qkv_w + dl_w:], ((0, 0), (0, lane_pad))).astype(BF16)
        *qkv_by_branch, dl, ba = _inproj_call(
            x, mod3, norm_attn_g[l].reshape(1, d), wa, wc, wz, wba, conv_w[l].astype(F32))

        o_list, lse_list = [], []
        for branch, ((_, dilation), qkv) in enumerate(zip(DILATED_BRANCHES, qkv_by_branch)):
            o, lse = _attn_call(qkv.reshape(bsz, qkv.shape[1], s, LANES), bias, branch, dilation)
            o_list.append(o)
            lse_list.append(lse)

        gate_pad = lambda v: jnp.pad(v.astype(F32), (N_DELTA_HEADS, LANES - 2 * N_DELTA_HEADS)).reshape(1, LANES)
        yd = _delta_call(dl, ba, gate_pad(a_log[l]), gate_pad(dt_bias[l]),
                         jnp.tile(delta_norm_g[l].astype(F32), N_DELTA_HEADS).reshape(1, DELTA_WIDTH))

        wo = w_out[l].astype(BF16)
        x = _out_ffn_call(
            x, o_list, lse_list, yd, mod3, wo[:ATTN_WIDTH], wo[ATTN_WIDTH:], norm_ffn_g[l].reshape(1, d),
            w_gate[l].astype(BF16), w_up[l].astype(BF16), w_down[l].astype(BF16), final_norm_g.reshape(1, d),
            final_norm=(l == depth - 1))
    return x
```

```python
import functools
import math

import jax
import jax.numpy as jnp
import numpy as np
from jax import lax
from jax.experimental import pallas as pl
from jax.experimental.pallas import tpu as pltpu

F32 = jnp.float32
BF16 = jnp.bfloat16

HEAD_DIM = 64
N_ATTN_HEADS = 8
N_DELTA_HEADS = 8
ATTN_WIDTH = N_ATTN_HEADS * HEAD_DIM
DELTA_WIDTH = N_DELTA_HEADS * HEAD_DIM
DILATED_BRANCHES = ((128, 1), (512, 4), (2048, 16))
BAND = 128
D4, D16 = DILATED_BRANCHES[1][1], DILATED_BRANCHES[2][1]
PAD_UNIT = 2048
N_BUCKETS = 32
MAX_DISTANCE = 2048
CONV_WIDTH = 4
CHUNK = 64
EPS = 1e-6
NEG_INF = -1e30

LANES = 128
MXU_COLS = 256
MAX_FAST_ROW_STRIDE = 4
VMEM_LIMIT_BYTES = 60000 * 1024

TOKEN_TILE = 512
HALO = 16
FFN_TOKEN_TILE = 512
FFN_ROW_GROUPS = 2
ATTN_UNROLL = 8
HEADS_PER_GROUP = 4
GROUP_W = HEADS_PER_GROUP * HEAD_DIM


def _dot(a, b):
    return jnp.dot(a, b, preferred_element_type=F32)


def _dot_nt(a, b):
    return lax.dot_general(a, b, (((1,), (1,)), ((), ())), preferred_element_type=F32)


def _split3(x):
    x1 = x.astype(BF16)
    r1 = x - x1.astype(F32)
    x2 = r1.astype(BF16)
    x3 = (r1 - x2.astype(F32)).astype(BF16)
    return x1, x2, x3


def _split2(x):
    x1 = x.astype(BF16)
    x2 = (x - x1.astype(F32)).astype(BF16)
    return x1, x2


def _sigmoid(x):
    return 1.0 / (1.0 + jnp.exp(-x))


def _silu(x):
    return x * _sigmoid(x)


def _head_of(idx):
    return lax.shift_right_logical(idx, int(math.log2(HEAD_DIM)))


def _within_head(idx):
    return lax.bitwise_and(idx, HEAD_DIM - 1)


def _compiler_params(semantics):
    return pltpu.CompilerParams(dimension_semantics=semantics, vmem_limit_bytes=VMEM_LIMIT_BYTES)


def _const_spec(shape):
    zeros = (0,) * len(shape)
    return pl.BlockSpec(shape, lambda *_: zeros, pipeline_mode=pl.Buffered(1))


def _mod_kernel(c_ref, w_ref, b_ref, o_ref):
    ca = _silu(c_ref[...])
    acc = b_ref[...]
    w = w_ref[...]
    for part in _split3(ca):
        for wpart in _split3(w):
            acc = acc + _dot(part, wpart)
    o_ref[...] = acc


def _mod_call(c, w_ada, b_ada):
    bsz, d = c.shape
    n = w_ada.shape[1]
    blk = d
    return pl.pallas_call(
        _mod_kernel,
        grid=(n // blk,),
        in_specs=[
            pl.BlockSpec((bsz, d), lambda j: (0, 0)),
            pl.BlockSpec((d, blk), lambda j: (0, j)),
            pl.BlockSpec((1, blk), lambda j: (0, j)),
        ],
        out_specs=pl.BlockSpec((bsz, blk), lambda j: (0, j)),
        out_shape=jax.ShapeDtypeStruct((bsz, n), F32),
        compiler_params=_compiler_params(("arbitrary",)),
        name="mod",
    )(c, w_ada, b_ada.reshape(1, n))


def _rmsnorm_rows(x, g):
    return x * lax.rsqrt(jnp.mean(x * x, axis=-1, keepdims=True) + EPS) * g


def _inproj_kernel(x_ref, xh_ref, mod_ref, g_ref, wa_ref, wc_ref, wz_ref, wba_ref, cw_ref,
                   qkv1_ref, qkv4_ref, qkv16_ref, dl_ref, ba_ref, hb_ref, *block_scratch):
    tm = x_ref.shape[1]
    conv_w = wc_ref.shape[1]
    n_attn_blocks = wa_ref.shape[1] // MXU_COLS
    stage_refs, pre_refs = block_scratch[:2 * n_attn_blocks], block_scratch[2 * n_attn_blocks:]
    xe = jnp.concatenate([xh_ref[0], x_ref[0]], axis=0)
    h = _rmsnorm_rows(xe, g_ref[...]) * (1.0 + mod_ref[0, 1:2, :]) + mod_ref[0, 0:1, :]
    hb_ref[...] = h.astype(BF16)

    n_slabs = wa_ref.shape[1] // LANES

    def attn_block(j0):
        blk = _dot(hb_ref[HALO:, :], wa_ref[:, j0 * LANES:j0 * LANES + MXU_COLS])
        stage_ref, stage4_ref = stage_refs[2 * (j0 * LANES // MXU_COLS):][:2]
        for j in range(j0, j0 + MXU_COLS // LANES):
            jj = j - j0
            slab = blk[:, jj * LANES:(jj + 1) * LANES]
            stage_ref[jj] = slab
            qkv1_ref[0, j] = slab.astype(BF16)
            for r in range(D4):
                cls = stage_ref[jj, pl.ds(r, tm // D4, stride=D4), :]
                stage4_ref[jj, r] = cls
                qkv4_ref[0, j, r] = cls.astype(BF16)
            for r in range(D16):
                sub = stage4_ref[jj, r % D4, pl.ds(r // D4, tm // D16, stride=D16 // D4), :]
                qkv16_ref[0, j, r] = sub.astype(BF16)

    row = lax.broadcasted_iota(jnp.int32, (tm + HALO, MXU_COLS), 0)
    no_history = (row < HALO) & (pl.program_id(1) == 0)

    def conv_block(c0):
        cols = slice(c0, c0 + MXU_COLS)
        pre = jnp.where(no_history, 0.0, _dot(hb_ref[...], wc_ref[:, cols]))
        pre_ref = pre_refs[c0 // MXU_COLS]
        pre_ref[...] = pre
        conv = pre[HALO:] * cw_ref[CONV_WIDTH - 1:CONV_WIDTH, cols]
        for back in range(1, CONV_WIDTH):
            tap = cw_ref[CONV_WIDTH - 1 - back:CONV_WIDTH - back, cols]
            conv = conv + pre_ref[HALO - back:HALO - back + tm, :] * tap
        dl_ref[0, :, cols] = _silu(conv)

    attn_starts = list(range(0, n_slabs, MXU_COLS // LANES))
    conv_starts = list(range(0, conv_w, MXU_COLS))
    while attn_starts or conv_starts:
        if conv_starts:
            conv_block(conv_starts.pop(0))
        if attn_starts:
            attn_block(attn_starts.pop(0))
    for c0 in range(0, wz_ref.shape[1], MXU_COLS):
        dl_ref[0, :, conv_w + c0:conv_w + c0 + MXU_COLS] = _dot(hb_ref[HALO:, :], wz_ref[:, c0:c0 + MXU_COLS])
    ba_ref[0] = _dot(hb_ref[HALO:, :], wba_ref[...])


def _inproj_call(x, mod3, g, wa, wc, wz, wba, conv_w):
    bsz, s, d = x.shape
    tm = TOKEN_TILE
    na, nc, nz, nba = wa.shape[1], wc.shape[1], wz.shape[1], wba.shape[1]
    n_slabs = na // LANES
    d4, d16 = DILATED_BRANCHES[1][1], DILATED_BRANCHES[2][1]
    row_spec = lambda w: pl.BlockSpec((1, tm, w), lambda b, i: (b, i, 0))
    cm_spec = lambda dil: pl.BlockSpec((1, n_slabs, dil, tm // dil, LANES), lambda b, i: (b, 0, 0, i, 0))
    cm_shape = lambda dil: jax.ShapeDtypeStruct((bsz, n_slabs, dil, s // dil, LANES), BF16)
    return pl.pallas_call(
        _inproj_kernel,
        grid=(bsz, s // tm),
        in_specs=[
            row_spec(d),
            pl.BlockSpec((1, HALO, d), lambda b, i: (b, jnp.maximum(i * (tm // HALO) - 1, 0), 0)),
            pl.BlockSpec((1, 6, d), lambda b, i: (b, 0, 0)),
            _const_spec((1, d)),
            _const_spec((d, na)),
            _const_spec((d, nc)),
            _const_spec((d, nz)),
            _const_spec((d, nba)),
            _const_spec(conv_w.shape),
        ],
        out_specs=[
            pl.BlockSpec((1, n_slabs, tm, LANES), lambda b, i: (b, 0, i, 0)),
            cm_spec(d4),
            cm_spec(d16),
            row_spec(nc + nz),
            row_spec(nba),
        ],
        out_shape=[
            jax.ShapeDtypeStruct((bsz, n_slabs, s, LANES), BF16),
            cm_shape(d4),
            cm_shape(d16),
            jax.ShapeDtypeStruct((bsz, s, nc + nz), F32),
            jax.ShapeDtypeStruct((bsz, s, nba), F32),
        ],
        scratch_shapes=(
            [pltpu.VMEM((tm + HALO, d), BF16)]
            + [pltpu.VMEM((MXU_COLS // LANES, tm, LANES), F32),
               pltpu.VMEM((MXU_COLS // LANES, D4, tm // D4, LANES), F32)] * (na // MXU_COLS)
            + [pltpu.VMEM((tm + HALO, MXU_COLS), F32)] * (nc // MXU_COLS)
        ),
        compiler_params=_compiler_params(("parallel", "parallel")),
        name="inproj",
    )(x, x, mod3, g, wa, wc, wz, wba, conv_w)


def _t5_bucket(distance):
    max_exact = N_BUCKETS // 2
    dist_f = jnp.maximum(distance, 1).astype(F32)
    large = max_exact + (jnp.log(dist_f / max_exact) / math.log(MAX_DISTANCE / max_exact)
                         * (N_BUCKETS - max_exact)).astype(jnp.int32)
    return jnp.where(distance < max_exact, distance, jnp.minimum(large, N_BUCKETS - 1))


def _bucket_index_tables():
    qi = jnp.arange(BAND)[:, None]
    kj = jnp.arange(2 * BAND)[None, :]
    steps = qi + BAND - kj
    in_window = (steps >= 0) & (steps <= BAND)
    tables = []
    for _, dilation in DILATED_BRANCHES:
        bucket = _t5_bucket(jnp.maximum(steps, 0) * dilation)
        v0 = jnp.where(in_window, bucket, -1)
        v1 = jnp.where(in_window & (kj >= BAND), bucket, -1)
        v2 = jnp.concatenate([v0[:, BAND:], jnp.full((BAND, BAND), -1, jnp.int32)], axis=1)
        tables.append(jnp.stack([v0, v1, v2]))
    return jnp.stack(tables).astype(jnp.int32)


def _bias_kernel(idx_ref, rb_ref, o_ref):
    idx = idx_ref[0]
    for h in range(N_ATTN_HEADS):
        acc = jnp.full(idx.shape, NEG_INF, F32)
        for b in range(N_BUCKETS):
            acc = jnp.where(idx == b, rb_ref[b, h], acc)
        o_ref[0, h] = acc


def _bias_call(idx_tables, rel_bias):
    nbr = idx_tables.shape[0]
    blk = idx_tables.shape[1:]
    return pl.pallas_call(
        _bias_kernel,
        grid=(nbr,),
        in_specs=[
            pl.BlockSpec((1,) + blk, lambda i: (i, 0, 0, 0)),
            pl.BlockSpec(memory_space=pltpu.SMEM),
        ],
        out_specs=pl.BlockSpec((1, N_ATTN_HEADS) + blk, lambda i: (i, 0, 0, 0, 0)),
        out_shape=jax.ShapeDtypeStruct((nbr, N_ATTN_HEADS) + blk, F32),
        compiler_params=_compiler_params(("arbitrary",)),
        name="bias",
    )(idx_tables, rel_bias.astype(F32))


def _attn_kernel(q_ref, k_ref, v_ref, bias_ref, o_ref, lse_ref, *, dilation, out_classes, blocks_per_class,
                 n_blocks):
    lane = lax.broadcasted_iota(jnp.int32, (BAND, LANES), 1)
    is_a = lane < HEAD_DIM
    class_shift = int(math.log2(blocks_per_class))
    out_stride = dilation // out_classes

    def body(jg, carry):
        blocks = []
        for u in range(ATTN_UNROLL):
            j = jg * ATTN_UNROLL + u
            r0 = pl.multiple_of(j * BAND, BAND)
            ks = pl.multiple_of(jnp.maximum(j - 1, 0) * BAND, BAND)
            n_in_class = lax.bitwise_and(j, blocks_per_class - 1)
            variant = jnp.where(n_in_class == 0, jnp.where(j > 0, 1, 2), 0)
            q = q_ref[0, 0, pl.ds(r0, BAND), :] * (HEAD_DIM ** -0.5)
            res_class = lax.shift_right_logical(j, class_shift)
            blocks.append(dict(
                q=q, kw=k_ref[0, 0, pl.ds(ks, 2 * BAND), :], vw=v_ref[0, 0, pl.ds(ks, 2 * BAND), :], variant=variant,
                out_class=lax.bitwise_and(res_class, out_classes - 1),
                row0=lax.shift_right_logical(res_class, int(math.log2(out_classes))) + out_stride * BAND * n_in_class))
        chains = [(blk, hh) for blk in blocks for hh in range(2)]
        zero = jnp.zeros((BAND, LANES), BF16)
        s_all = [_dot_nt(jnp.where(is_a if hh == 0 else jnp.logical_not(is_a), blk["q"], zero), blk["kw"])
                 + bias_ref[0, hh, blk["variant"]] for blk, hh in chains]
        m_all = [jnp.max(s, axis=-1, keepdims=True) for s in s_all]
        e_all = [jnp.exp(s - m) for s, m in zip(s_all, m_all)]
        d_all = [jnp.sum(e, axis=-1, keepdims=True) for e in e_all]
        o_all = [_dot(e.astype(BF16), blk["vw"]) / d for e, d, (blk, _) in zip(e_all, d_all, chains)]
        lse_all = [jnp.broadcast_to(m + jnp.log(d), (BAND, LANES)) for m, d in zip(m_all, d_all)]
        for u, blk in enumerate(blocks):
            rows = pl.ds(blk["row0"], BAND, stride=out_stride)
            o_ref[0, blk["out_class"], rows, :] = jnp.where(is_a, o_all[2 * u], o_all[2 * u + 1])
            lse_ref[0, blk["out_class"], rows, :] = jnp.where(is_a, lse_all[2 * u], lse_all[2 * u + 1])
        return carry

    lax.fori_loop(0, n_blocks // ATTN_UNROLL, body, 0)


def _attn_call(qkv, bias, branch, dilation, out_classes):
    bsz, n_slabs, s, _ = qkv.shape
    hp = n_slabs // 3
    n_blocks = s // BAND
    blocks_per_class = n_blocks // dilation
    assert blocks_per_class & (blocks_per_class - 1) == 0 and n_blocks % ATTN_UNROLL == 0
    assert dilation % out_classes == 0 and out_classes & (out_classes - 1) == 0
    kern = functools.partial(_attn_kernel, dilation=dilation, out_classes=out_classes,
                             blocks_per_class=blocks_per_class, n_blocks=n_blocks)
    slab_spec = lambda t: pl.BlockSpec((1, 1, s, LANES), lambda h, b: (b, t * hp + h, 0, 0))
    out_spec = pl.BlockSpec((1, out_classes, s // out_classes, LANES), lambda h, b: (b, 0, 0, h))
    return pl.pallas_call(
        kern,
        grid=(hp, bsz),
        in_specs=[
            slab_spec(0), slab_spec(1), slab_spec(2),
            pl.BlockSpec((1, 2, 3, BAND, 2 * BAND), lambda h, b: (branch, h, 0, 0, 0)),
        ],
        out_specs=[out_spec, out_spec],
        out_shape=[jax.ShapeDtypeStruct((bsz, out_classes, s // out_classes, hp * LANES), F32)] * 2,
        compiler_params=_compiler_params(("parallel", "parallel")),
        name=f"attn_d{dilation}",
    )(qkv, qkv, qkv, bias)


def _bd_stack(x):
    xb = x.astype(BF16)
    head_of_lane = _head_of(lax.broadcasted_iota(jnp.int32, xb.shape, 1))
    zero = jnp.zeros_like(xb)
    return jnp.concatenate([jnp.where(head_of_lane == h, xb, zero) for h in range(HEADS_PER_GROUP)], axis=0)


def _delta_kernel(dl_ref, ba_ref, alog_ref, dtb_ref, ng_ref, o_ref, state_ref):
    bsz = dl_ref.shape[0]
    tc = bsz * CHUNK
    qkv_w = 3 * DELTA_WIDTH
    n_groups = DELTA_WIDTH // GROUP_W

    @pl.when(pl.program_id(0) == 0)
    def _():
        state_ref[...] = jnp.zeros_like(state_ref)

    q_raw = dl_ref[:, :, :DELTA_WIDTH].reshape(tc, DELTA_WIDTH)
    k_raw = dl_ref[:, :, DELTA_WIDTH:2 * DELTA_WIDTH].reshape(tc, DELTA_WIDTH)
    v_all = dl_ref[:, :, 2 * DELTA_WIDTH:qkv_w].reshape(tc, DELTA_WIDTH)

    r512 = _head_of(lax.broadcasted_iota(jnp.int32, (DELTA_WIDTH, DELTA_WIDTH), 0))
    c512 = _head_of(lax.broadcasted_iota(jnp.int32, (DELTA_WIDTH, DELTA_WIDTH), 1))
    head_ones = (r512 == c512).astype(BF16)

    def head_sum(x):
        return _dot(x.astype(BF16), head_ones)

    q_all = q_raw * lax.rsqrt(head_sum(q_raw * q_raw) + EPS) * (HEAD_DIM ** -0.5)
    k_all = k_raw * lax.rsqrt(head_sum(k_raw * k_raw) + EPS)

    t = ba_ref[...].reshape(tc, LANES)
    za = t + dtb_ref[...]
    softplus = jnp.maximum(za, 0.0) + jnp.log1p(jnp.exp(-jnp.abs(za)))
    g_lane = -jnp.exp(alog_ref[...]) * softplus
    er = lax.broadcasted_iota(jnp.int32, (LANES, DELTA_WIDTH), 0)
    ec = _head_of(lax.broadcasted_iota(jnp.int32, (LANES, DELTA_WIDTH), 1))
    expand_beta = (er == ec).astype(BF16)
    expand_g = (er == ec + N_DELTA_HEADS).astype(BF16)
    beta_all = _dot(_sigmoid(t).astype(BF16), expand_beta)
    g_all = jnp.zeros((tc, DELTA_WIDTH), F32)
    for part in _split3(g_lane):
        g_all = g_all + _dot(part, expand_g)

    ci = lax.broadcasted_iota(jnp.int32, (CHUNK, CHUNK), 0)
    cj = lax.broadcasted_iota(jnp.int32, (CHUNK, CHUNK), 1)
    lower_incl = (ci >= cj).astype(BF16)
    all_ones = jnp.ones((CHUNK, CHUNK), BF16)
    row_w = lax.broadcasted_iota(jnp.int32, (CHUNK, DELTA_WIDTH), 0)
    col_w = _within_head(lax.broadcasted_iota(jnp.int32, (CHUNK, DELTA_WIDTH), 1))
    upper_incl_w = row_w <= col_w
    row_g = lax.broadcasted_iota(jnp.int32, (CHUNK, GROUP_W), 0)
    col_g = _within_head(lax.broadcasted_iota(jnp.int32, (CHUNK, GROUP_W), 1))
    causal_g = row_g >= col_g
    strict_g = row_g > col_g
    eye_g = (row_g == col_g).astype(F32)
    level_masks = []
    for n in range(int(math.log2(CHUNK))):
        shr = lambda v, k: lax.shift_right_logical(v, k)
        level_masks.append((shr(row_g, n + 1) == shr(col_g, n + 1)) & (shr(row_g, n) > shr(col_g, n)))
    sr = _head_of(lax.broadcasted_iota(jnp.int32, (GROUP_W, GROUP_W), 0))
    sc = _head_of(lax.broadcasted_iota(jnp.int32, (GROUP_W, GROUP_W), 1))
    state_mask = sr == sc

    chains = []
    for c in range(bsz):
        rows = slice(c * CHUNK, (c + 1) * CHUNK)
        gb = g_all[rows]
        gcum = jnp.zeros((CHUNK, DELTA_WIDTH), F32)
        gcum_t = jnp.zeros((CHUNK, DELTA_WIDTH), F32)
        for part in _split3(gb):
            gcum = gcum + _dot(lower_incl, part)
        for part in _split3(jnp.where(upper_incl_w, gb, 0.0)):
            gcum_t = gcum_t + _dot(all_ones, part)
        g_last = gcum[CHUNK - 1:CHUNK, :]
        exp_g = jnp.exp(gcum)
        exp_dec = jnp.exp(g_last - gcum)
        exp_last = jnp.exp(g_last)

        kc, qc, vc, bb = k_all[rows], q_all[rows], v_all[rows], beta_all[rows]
        kb = kc * bb
        vb = vc * bb
        kbe = kb * exp_g
        qe = qc * exp_g
        kdec = kc * exp_dec
        for g in range(n_groups):
            gl = slice(g * GROUP_W, (g + 1) * GROUP_W)
            diff = gcum[:, gl] - gcum_t[:, gl]
            decay = jnp.where(causal_g, jnp.exp(jnp.where(causal_g, diff, 0.0)), 0.0)
            aq = _dot_nt(jnp.concatenate([kb[:, gl], qc[:, gl]], axis=0).astype(BF16), _bd_stack(kc[:, gl]))
            chains.append(dict(
                slot=c * n_groups + g, a=jnp.where(strict_g, aq[:CHUNK] * decay, 0.0),
                qk=(aq[CHUNK:] * decay).astype(BF16), qe=qe[:, gl], kdec_t=kdec[:, gl].T.astype(BF16),
                exp_last=exp_last[:, gl],
                rhs=jnp.concatenate([_bd_stack(vb[:, gl]), _bd_stack(kbe[:, gl])], axis=1)))

    t_invs = [eye_g - jnp.where(level_masks[0], ch["a"], 0.0) for ch in chains]
    for mask in level_masks[1:]:
        ys = [_dot(jnp.where(mask, ch["a"], 0.0).astype(BF16), _bd_stack(t)) for ch, t in zip(chains, t_invs)]
        t_invs = [t - _dot(t.astype(BF16), _bd_stack(y)) for t, y in zip(t_invs, ys)]
    uws = [_dot(t.astype(BF16), ch["rhs"]) for ch, t in zip(chains, t_invs)]

    states = [state_ref[ch["slot"]] for ch in chains]
    wss = [_dot(jnp.concatenate([uw[:, GROUP_W:], ch["qe"]], axis=0).astype(BF16), st.astype(BF16))
           for ch, uw, st in zip(chains, uws, states)]
    v_news = [uw[:, :GROUP_W] - ws[:CHUNK] for uw, ws in zip(uws, wss)]
    outs = [ws[CHUNK:] + _dot(ch["qk"], _bd_stack(vn)) for ch, ws, vn in zip(chains, wss, v_news)]
    upds = [_dot(ch["kdec_t"], vn.astype(BF16)) for ch, vn in zip(chains, v_news)]
    for ch, st, upd in zip(chains, states, upds):
        state_ref[ch["slot"]] = st * ch["exp_last"] + jnp.where(state_mask, upd, 0.0)

    o_all = jnp.concatenate(
        [jnp.concatenate(outs[c * n_groups:(c + 1) * n_groups], axis=1) for c in range(bsz)], axis=0)
    ms = head_sum(o_all * o_all) * (1.0 / HEAD_DIM)
    z = dl_ref[:, :, qkv_w:].reshape(tc, DELTA_WIDTH)
    o_ref[...] = (o_all * lax.rsqrt(ms + EPS) * ng_ref[...] * _silu(z)).reshape(bsz, CHUNK, DELTA_WIDTH)


def _delta_call(dl, ba, alog_row, dtb_row, ng_row):
    bsz, s, wd = dl.shape
    n_groups = DELTA_WIDTH // GROUP_W
    chunk_spec = lambda w: pl.BlockSpec((bsz, CHUNK, w), lambda i: (0, i, 0))
    return pl.pallas_call(
        _delta_kernel,
        grid=(s // CHUNK,),
        in_specs=[
            chunk_spec(wd),
            chunk_spec(LANES),
            _const_spec((1, LANES)),
            _const_spec((1, LANES)),
            _const_spec((1, DELTA_WIDTH)),
        ],
        out_specs=chunk_spec(DELTA_WIDTH),
        out_shape=jax.ShapeDtypeStruct((bsz, s, DELTA_WIDTH), F32),
        scratch_shapes=[pltpu.VMEM((bsz * n_groups, GROUP_W, GROUP_W), F32)],
        compiler_params=_compiler_params(("arbitrary",)),
        name="delta",
    )(dl, ba, alog_row, dtb_row, ng_row)


def _out_ffn_kernel(x_ref, o1_ref, o2_ref, o3_ref, l1_ref, l2_ref, l3_ref, yd_ref, mod_ref,
                    wtop_ref, wbot_ref, g2_ref, wg_ref, wu_ref, wdn_ref, gf_ref, out_ref, natural_ref, *, final_norm):
    tm = x_ref.shape[1]
    groups = [slice(r, r + tm // FFN_ROW_GROUPS) for r in range(0, tm, tm // FFN_ROW_GROUPS)]

    staged = 0

    def natural_rows(ref):
        nonlocal staged
        n_classes = ref.shape[1]
        if n_classes == 1:
            return lambda rows: ref[0, 0, rows, :]
        slot = staged
        staged += 1
        n_slabs = ref.shape[3] // LANES
        for sl in range(n_slabs):
            for cls in range(n_classes):
                natural_ref[slot, sl, pl.ds(cls, tm // n_classes, stride=n_classes), :] = (
                    ref[0, cls, :, sl * LANES:(sl + 1) * LANES])
        return lambda rows: jnp.concatenate([natural_ref[slot, sl, rows, :] for sl in range(n_slabs)], axis=1)

    o_rows = [natural_rows(r) for r in (o1_ref, o2_ref, o3_ref)]
    l_rows = [natural_rows(r) for r in (l1_ref, l2_ref, l3_ref)]

    def merge(rows):
        lses = [get(rows) for get in l_rows]
        outs = [get(rows) for get in o_rows]
        mx = jnp.maximum(jnp.maximum(lses[0], lses[1]), lses[2])
        ws = [jnp.exp(l - mx) for l in lses]
        den = ws[0] + ws[1] + ws[2]
        return ((ws[0] * outs[0] + ws[1] * outs[1] + ws[2] * outs[2]) / den).astype(BF16)

    ya = [merge(rows) for rows in groups]
    y = [_dot(a, wtop_ref[...]) + _dot(yd_ref[0, rows, :].astype(BF16), wbot_ref[...]) for a, rows in zip(ya, groups)]
    x1 = [x_ref[0, rows, :] + mod_ref[0, 2:3, :] * yy for yy, rows in zip(y, groups)]
    hb = [(_rmsnorm_rows(xx, g2_ref[...]) * (1.0 + mod_ref[0, 4:5, :]) + mod_ref[0, 3:4, :]).astype(BF16) for xx in x1]
    gate = [_dot(h, wg_ref[...]) for h in hb]
    up = [_dot(h, wu_ref[...]) for h in hb]
    y2 = [_dot((_silu(g) * u).astype(BF16), wdn_ref[...]) for g, u in zip(gate, up)]
    for rows, xx, yy in zip(groups, x1, y2):
        x2 = xx + mod_ref[0, 5:6, :] * yy
        out_ref[0, rows, :] = _rmsnorm_rows(x2, gf_ref[...]) if final_norm else x2


def _out_ffn_call(x, o_list, lse_list, yd, mod3, wtop, wbot, g2, wg, wu, wdn, gf, final_norm):
    bsz, s, d = x.shape
    tm = FFN_TOKEN_TILE
    aw = o_list[0].shape[-1]
    dff = wg.shape[1]
    row_spec = lambda w: pl.BlockSpec((1, tm, w), lambda b, i: (b, i, 0))
    branch_spec = lambda t: pl.BlockSpec((1, t.shape[1], tm // t.shape[1], aw), lambda b, i: (b, 0, i, 0))
    n_staged = sum(t.shape[1] > 1 for t in o_list + lse_list)
    return pl.pallas_call(
        functools.partial(_out_ffn_kernel, final_norm=final_norm),
        grid=(bsz, s // tm),
        in_specs=[row_spec(d)] + [branch_spec(t) for t in o_list + lse_list] + [row_spec(yd.shape[-1])] + [
            pl.BlockSpec((1, 6, d), lambda b, i: (b, 0, 0)),
            _const_spec(wtop.shape), _const_spec(wbot.shape), _const_spec((1, d)),
            _const_spec((d, dff)), _const_spec((d, dff)), _const_spec((dff, d)), _const_spec((1, d)),
        ],
        out_specs=row_spec(d),
        out_shape=jax.ShapeDtypeStruct((bsz, s, d), x.dtype),
        scratch_shapes=[pltpu.VMEM((max(n_staged, 1), aw // LANES, tm, LANES), F32)],
        compiler_params=_compiler_params(("parallel", "parallel")),
        name="out_ffn",
    )(x, *o_list, *lse_list, yd, mod3, wtop, wbot, g2, wg, wu, wdn, gf)


def kernel(x, c, w_ada, b_ada, norm_attn_g, w_in, rel_bias, conv_w, a_log, dt_bias, delta_norm_g, w_out,
           norm_ffn_g, w_gate, w_up, w_down, final_norm_g):
    bsz, s, d = x.shape
    depth = w_ada.shape[0]
    assert s % PAD_UNIT == 0 and s % CHUNK == 0 and s % TOKEN_TILE == 0
    assert all(window // dilation == BAND for window, dilation in DILATED_BRANCHES)

    bias = _bias_call(_bucket_index_tables(), rel_bias)
    lane_pad = LANES - 2 * N_DELTA_HEADS
    for l in range(depth):
        mod3 = _mod_call(c, w_ada[l], b_ada[l]).reshape(bsz, 6, d)
        qkv_w = 3 * ATTN_WIDTH
        dl_w = 4 * DELTA_WIDTH
        conv_cols = 3 * DELTA_WIDTH
        wa = w_in[l][:, :qkv_w].astype(BF16)
        wc = w_in[l][:, qkv_w:qkv_w + conv_cols].astype(BF16)
        wz = w_in[l][:, qkv_w + conv_cols:qkv_w + dl_w].astype(BF16)
        wba = jnp.pad(w_in[l][:, qkv_w + dl_w:], ((0, 0), (0, lane_pad))).astype(BF16)
        *qkv_by_branch, dl, ba = _inproj_call(
            x, mod3, norm_attn_g[l].reshape(1, d), wa, wc, wz, wba, conv_w[l].astype(F32))

        o_list, lse_list = [], []
        for branch, ((_, dilation), qkv) in enumerate(zip(DILATED_BRANCHES, qkv_by_branch)):
            out_classes = max(dilation // MAX_FAST_ROW_STRIDE, 1)
            o, lse = _attn_call(qkv.reshape(bsz, qkv.shape[1], s, LANES), bias, branch, dilation, out_classes)
            o_list.append(o)
            lse_list.append(lse)

        gate_pad = lambda v: jnp.pad(v.astype(F32), (N_DELTA_HEADS, LANES - 2 * N_DELTA_HEADS)).reshape(1, LANES)
        yd = _delta_call(dl, ba, gate_pad(a_log[l]), gate_pad(dt_bias[l]),
                         jnp.tile(delta_norm_g[l].astype(F32), N_DELTA_HEADS).reshape(1, DELTA_WIDTH))

        wo = w_out[l].astype(BF16)
        x = _out_ffn_call(
            x, o_list, lse_list, yd, mod3, wo[:ATTN_WIDTH], wo[ATTN_WIDTH:], norm_ffn_g[l].reshape(1, d),
            w_gate[l].astype(BF16), w_up[l].astype(BF16), w_down[l].astype(BF16), final_norm_g.reshape(1, d),
            final_norm=(l == depth - 1))
    return x
```

```python
import functools
import math

import jax
import jax.numpy as jnp
import numpy as np
from jax import lax
from jax.experimental import pallas as pl
from jax.experimental.pallas import tpu as pltpu

F32 = jnp.float32
BF16 = jnp.bfloat16

HEAD_DIM = 64
N_ATTN_HEADS = 8
N_DELTA_HEADS = 8
ATTN_WIDTH = N_ATTN_HEADS * HEAD_DIM
DELTA_WIDTH = N_DELTA_HEADS * HEAD_DIM
DILATED_BRANCHES = ((128, 1), (512, 4), (2048, 16))
BAND = 128
D4, D16 = DILATED_BRANCHES[1][1], DILATED_BRANCHES[2][1]
PAD_UNIT = 2048
N_BUCKETS = 32
MAX_DISTANCE = 2048
CONV_WIDTH = 4
CHUNK = 64
EPS = 1e-6
NEG_INF = -1e30

LANES = 128
MXU_COLS = 256
MAX_FAST_ROW_STRIDE = 4
VMEM_LIMIT_BYTES = 60000 * 1024

TOKEN_TILE = 512
HALO = 16
FFN_TOKEN_TILE = 512
FFN_ROW_GROUPS = 2
ATTN_UNROLL = 8
HEADS_PER_GROUP = 4
GROUP_W = HEADS_PER_GROUP * HEAD_DIM


def _dot(a, b):
    return jnp.dot(a, b, preferred_element_type=F32)


def _dot_nt(a, b):
    return lax.dot_general(a, b, (((1,), (1,)), ((), ())), preferred_element_type=F32)


def _split3(x):
    x1 = x.astype(BF16)
    r1 = x - x1.astype(F32)
    x2 = r1.astype(BF16)
    x3 = (r1 - x2.astype(F32)).astype(BF16)
    return x1, x2, x3


def _split2(x):
    x1 = x.astype(BF16)
    x2 = (x - x1.astype(F32)).astype(BF16)
    return x1, x2


def _sigmoid(x):
    return 1.0 / (1.0 + jnp.exp(-x))


def _silu(x):
    return x * _sigmoid(x)


def _head_of(idx):
    return lax.shift_right_logical(idx, int(math.log2(HEAD_DIM)))


def _within_head(idx):
    return lax.bitwise_and(idx, HEAD_DIM - 1)


def _compiler_params(semantics):
    return pltpu.CompilerParams(dimension_semantics=semantics, vmem_limit_bytes=VMEM_LIMIT_BYTES)


def _const_spec(shape):
    zeros = (0,) * len(shape)
    return pl.BlockSpec(shape, lambda *_: zeros, pipeline_mode=pl.Buffered(1))


def _mod_kernel(c_ref, w_ref, b_ref, o_ref):
    ca = _silu(c_ref[...])
    acc = b_ref[...]
    w = w_ref[...]
    for part in _split2(ca):
        for wpart in _split2(w):
            acc = acc + _dot(part, wpart)
    o_ref[...] = acc


def _mod_call(c, w_ada, b_ada):
    bsz, d = c.shape
    n = w_ada.shape[1]
    blk = d
    return pl.pallas_call(
        _mod_kernel,
        grid=(n // blk,),
        in_specs=[
            pl.BlockSpec((bsz, d), lambda j: (0, 0)),
            pl.BlockSpec((d, blk), lambda j: (0, j)),
            pl.BlockSpec((1, blk), lambda j: (0, j)),
        ],
        out_specs=pl.BlockSpec((bsz, blk), lambda j: (0, j)),
        out_shape=jax.ShapeDtypeStruct((bsz, n), F32),
        compiler_params=_compiler_params(("arbitrary",)),
        name="mod",
    )(c, w_ada, b_ada.reshape(1, n))


def _rmsnorm_rows(x, g):
    return x * lax.rsqrt(jnp.mean(x * x, axis=-1, keepdims=True) + EPS) * g


def _inproj_kernel(x_ref, xh_ref, mod_ref, g_ref, wa_ref, wc_ref, wz_ref, wba_ref, cw_ref,
                   qkv1_ref, qkv4_ref, qkv16_ref, dl_ref, ba_ref, hb_ref, *block_scratch):
    tm = x_ref.shape[1]
    conv_w = wc_ref.shape[1]
    n_attn_blocks = wa_ref.shape[1] // MXU_COLS
    stage_refs, pre_refs = block_scratch[:2 * n_attn_blocks], block_scratch[2 * n_attn_blocks:]
    xe = jnp.concatenate([xh_ref[0], x_ref[0]], axis=0)
    h = _rmsnorm_rows(xe, g_ref[...]) * (1.0 + mod_ref[0, 1:2, :]) + mod_ref[0, 0:1, :]
    hb_ref[...] = h.astype(BF16)

    n_slabs = wa_ref.shape[1] // LANES

    def attn_block(j0):
        blk = _dot(hb_ref[HALO:, :], wa_ref[:, j0 * LANES:j0 * LANES + MXU_COLS])
        stage_ref, stage4_ref = stage_refs[2 * (j0 * LANES // MXU_COLS):][:2]
        for j in range(j0, j0 + MXU_COLS // LANES):
            jj = j - j0
            slab = blk[:, jj * LANES:(jj + 1) * LANES]
            stage_ref[jj] = slab
            qkv1_ref[0, j] = slab.astype(BF16)
            for r in range(D4):
                cls = stage_ref[jj, pl.ds(r, tm // D4, stride=D4), :]
                stage4_ref[jj, r] = cls
                qkv4_ref[0, j, r] = cls.astype(BF16)
            for r in range(D16):
                sub = stage4_ref[jj, r % D4, pl.ds(r // D4, tm // D16, stride=D16 // D4), :]
                qkv16_ref[0, j, r] = sub.astype(BF16)

    row = lax.broadcasted_iota(jnp.int32, (tm + HALO, MXU_COLS), 0)
    no_history = (row < HALO) & (pl.program_id(1) == 0)

    def conv_block(c0):
        cols = slice(c0, c0 + MXU_COLS)
        pre = jnp.where(no_history, 0.0, _dot(hb_ref[...], wc_ref[:, cols]))
        pre_ref = pre_refs[c0 // MXU_COLS]
        pre_ref[...] = pre
        conv = pre[HALO:] * cw_ref[CONV_WIDTH - 1:CONV_WIDTH, cols]
        for back in range(1, CONV_WIDTH):
            tap = cw_ref[CONV_WIDTH - 1 - back:CONV_WIDTH - back, cols]
            conv = conv + pre_ref[HALO - back:HALO - back + tm, :] * tap
        dl_ref[0, :, cols] = _silu(conv)

    attn_starts = list(range(0, n_slabs, MXU_COLS // LANES))
    conv_starts = list(range(0, conv_w, MXU_COLS))
    while attn_starts or conv_starts:
        if conv_starts:
            conv_block(conv_starts.pop(0))
        if attn_starts:
            attn_block(attn_starts.pop(0))
    for c0 in range(0, wz_ref.shape[1], MXU_COLS):
        dl_ref[0, :, conv_w + c0:conv_w + c0 + MXU_COLS] = _dot(hb_ref[HALO:, :], wz_ref[:, c0:c0 + MXU_COLS])
    ba_ref[0] = _dot(hb_ref[HALO:, :], wba_ref[...])


def _inproj_call(x, mod3, g, wa, wc, wz, wba, conv_w):
    bsz, s, d = x.shape
    tm = TOKEN_TILE
    na, nc, nz, nba = wa.shape[1], wc.shape[1], wz.shape[1], wba.shape[1]
    n_slabs = na // LANES
    d4, d16 = DILATED_BRANCHES[1][1], DILATED_BRANCHES[2][1]
    row_spec = lambda w: pl.BlockSpec((1, tm, w), lambda b, i: (b, i, 0))
    cm_spec = lambda dil: pl.BlockSpec((1, n_slabs, dil, tm // dil, LANES), lambda b, i: (b, 0, 0, i, 0))
    cm_shape = lambda dil: jax.ShapeDtypeStruct((bsz, n_slabs, dil, s // dil, LANES), BF16)
    return pl.pallas_call(
        _inproj_kernel,
        grid=(bsz, s // tm),
        in_specs=[
            row_spec(d),
            pl.BlockSpec((1, HALO, d), lambda b, i: (b, jnp.maximum(i * (tm // HALO) - 1, 0), 0)),
            pl.BlockSpec((1, 6, d), lambda b, i: (b, 0, 0)),
            _const_spec((1, d)),
            _const_spec((d, na)),
            _const_spec((d, nc)),
            _const_spec((d, nz)),
            _const_spec((d, nba)),
            _const_spec(conv_w.shape),
        ],
        out_specs=[
            pl.BlockSpec((1, n_slabs, tm, LANES), lambda b, i: (b, 0, i, 0)),
            cm_spec(d4),
            cm_spec(d16),
            row_spec(nc + nz),
            row_spec(nba),
        ],
        out_shape=[
            jax.ShapeDtypeStruct((bsz, n_slabs, s, LANES), BF16),
            cm_shape(d4),
            cm_shape(d16),
            jax.ShapeDtypeStruct((bsz, s, nc + nz), F32),
            jax.ShapeDtypeStruct((bsz, s, nba), F32),
        ],
        scratch_shapes=(
            [pltpu.VMEM((tm + HALO, d), BF16)]
            + [pltpu.VMEM((MXU_COLS // LANES, tm, LANES), F32),
               pltpu.VMEM((MXU_COLS // LANES, D4, tm // D4, LANES), F32)] * (na // MXU_COLS)
            + [pltpu.VMEM((tm + HALO, MXU_COLS), F32)] * (nc // MXU_COLS)
        ),
        compiler_params=_compiler_params(("parallel", "parallel")),
        name="inproj",
    )(x, x, mod3, g, wa, wc, wz, wba, conv_w)


def _t5_bucket(distance):
    max_exact = N_BUCKETS // 2
    dist_f = jnp.maximum(distance, 1).astype(F32)
    large = max_exact + (jnp.log(dist_f / max_exact) / math.log(MAX_DISTANCE / max_exact)
                         * (N_BUCKETS - max_exact)).astype(jnp.int32)
    return jnp.where(distance < max_exact, distance, jnp.minimum(large, N_BUCKETS - 1))


def _bucket_index_tables():
    qi = jnp.arange(BAND)[:, None]
    kj = jnp.arange(2 * BAND)[None, :]
    steps = qi + BAND - kj
    in_window = (steps >= 0) & (steps <= BAND)
    tables = [jnp.where(in_window, _t5_bucket(jnp.maximum(steps, 0) * dilation), -1)
              for _, dilation in DILATED_BRANCHES]
    return jnp.stack(tables).astype(jnp.int32)


def _bias_kernel(idx_ref, rb_ref, o_ref):
    idx = idx_ref[0]
    in_current = lax.broadcasted_iota(jnp.int32, idx.shape, 1) >= BAND
    masked_half = jnp.full((BAND, BAND), NEG_INF, F32)
    for h in range(N_ATTN_HEADS):
        acc = jnp.full(idx.shape, NEG_INF, F32)
        for b in range(N_BUCKETS):
            acc = jnp.where(idx == b, rb_ref[b, h], acc)
        o_ref[0, h, 0] = acc
        o_ref[0, h, 1] = jnp.where(in_current, acc, NEG_INF)
        o_ref[0, h, 2] = jnp.concatenate([acc[:, BAND:], masked_half], axis=1)


def _bias_call(idx_tables, rel_bias):
    nbr = idx_tables.shape[0]
    blk = (3,) + idx_tables.shape[1:]
    return pl.pallas_call(
        _bias_kernel,
        grid=(nbr,),
        in_specs=[
            pl.BlockSpec((1,) + idx_tables.shape[1:], lambda i: (i, 0, 0)),
            pl.BlockSpec(memory_space=pltpu.SMEM),
        ],
        out_specs=pl.BlockSpec((1, N_ATTN_HEADS) + blk, lambda i: (i, 0, 0, 0, 0)),
        out_shape=jax.ShapeDtypeStruct((nbr, N_ATTN_HEADS) + blk, F32),
        compiler_params=_compiler_params(("arbitrary",)),
        name="bias",
    )(idx_tables, rel_bias.astype(F32))


def _attn_kernel(q_ref, k_ref, v_ref, bias_ref, o_ref, lse_ref, *, dilation, out_classes, blocks_per_class,
                 n_blocks):
    lane = lax.broadcasted_iota(jnp.int32, (BAND, LANES), 1)
    is_a = lane < HEAD_DIM
    class_shift = int(math.log2(blocks_per_class))
    out_stride = dilation // out_classes

    def body(jg, carry):
        blocks = []
        for u in range(ATTN_UNROLL):
            j = jg * ATTN_UNROLL + u
            r0 = pl.multiple_of(j * BAND, BAND)
            ks = pl.multiple_of(jnp.maximum(j - 1, 0) * BAND, BAND)
            n_in_class = lax.bitwise_and(j, blocks_per_class - 1)
            variant = jnp.where(n_in_class == 0, jnp.where(j > 0, 1, 2), 0)
            q = q_ref[0, 0, pl.ds(r0, BAND), :] * (HEAD_DIM ** -0.5)
            res_class = lax.shift_right_logical(j, class_shift)
            blocks.append(dict(
                q=q, kw=k_ref[0, 0, pl.ds(ks, 2 * BAND), :], vw=v_ref[0, 0, pl.ds(ks, 2 * BAND), :], variant=variant,
                out_class=lax.bitwise_and(res_class, out_classes - 1),
                row0=lax.shift_right_logical(res_class, int(math.log2(out_classes))) + out_stride * BAND * n_in_class))
        chains = [(blk, hh) for blk in blocks for hh in range(2)]
        zero = jnp.zeros((BAND, LANES), BF16)
        s_all = [_dot_nt(jnp.where(is_a if hh == 0 else jnp.logical_not(is_a), blk["q"], zero), blk["kw"])
                 + bias_ref[0, hh, blk["variant"]] for blk, hh in chains]
        m_all = [jnp.max(s, axis=-1, keepdims=True) for s in s_all]
        e_all = [jnp.exp(s - m) for s, m in zip(s_all, m_all)]
        d_all = [jnp.sum(e, axis=-1, keepdims=True) for e in e_all]
        o_all = [_dot(e.astype(BF16), blk["vw"]) / d for e, d, (blk, _) in zip(e_all, d_all, chains)]
        lse_all = [jnp.broadcast_to(m + jnp.log(d), (BAND, LANES)) for m, d in zip(m_all, d_all)]
        for u, blk in enumerate(blocks):
            rows = pl.ds(blk["row0"], BAND, stride=out_stride)
            o_ref[0, blk["out_class"], rows, :] = jnp.where(is_a, o_all[2 * u], o_all[2 * u + 1])
            lse_ref[0, blk["out_class"], rows, :] = jnp.where(is_a, lse_all[2 * u], lse_all[2 * u + 1])
        return carry

    lax.fori_loop(0, n_blocks // ATTN_UNROLL, body, 0)


def _attn_call(qkv, bias, branch, dilation, out_classes):
    bsz, n_slabs, s, _ = qkv.shape
    hp = n_slabs // 3
    n_blocks = s // BAND
    blocks_per_class = n_blocks // dilation
    assert blocks_per_class & (blocks_per_class - 1) == 0 and n_blocks % ATTN_UNROLL == 0
    assert dilation % out_classes == 0 and out_classes & (out_classes - 1) == 0
    kern = functools.partial(_attn_kernel, dilation=dilation, out_classes=out_classes,
                             blocks_per_class=blocks_per_class, n_blocks=n_blocks)
    slab_spec = lambda t: pl.BlockSpec((1, 1, s, LANES), lambda h, b: (b, t * hp + h, 0, 0))
    out_spec = pl.BlockSpec((1, out_classes, s // out_classes, LANES), lambda h, b: (b, 0, 0, h))
    return pl.pallas_call(
        kern,
        grid=(hp, bsz),
        in_specs=[
            slab_spec(0), slab_spec(1), slab_spec(2),
            pl.BlockSpec((1, 2, 3, BAND, 2 * BAND), lambda h, b: (branch, h, 0, 0, 0)),
        ],
        out_specs=[out_spec, out_spec],
        out_shape=[jax.ShapeDtypeStruct((bsz, out_classes, s // out_classes, hp * LANES), F32)] * 2,
        compiler_params=_compiler_params(("parallel", "parallel")),
        name=f"attn_d{dilation}",
    )(qkv, qkv, qkv, bias)


def _bd_stack(x):
    xb = x.astype(BF16)
    head_of_lane = _head_of(lax.broadcasted_iota(jnp.int32, xb.shape, 1))
    zero = jnp.zeros_like(xb)
    return jnp.concatenate([jnp.where(head_of_lane == h, xb, zero) for h in range(HEADS_PER_GROUP)], axis=0)


def _delta_kernel(dl_ref, ba_ref, alog_ref, dtb_ref, ng_ref, o_ref, state_ref):
    bsz = dl_ref.shape[0]
    tc = bsz * CHUNK
    qkv_w = 3 * DELTA_WIDTH
    n_groups = DELTA_WIDTH // GROUP_W

    @pl.when(pl.program_id(0) == 0)
    def _():
        state_ref[...] = jnp.zeros_like(state_ref)

    q_raw = dl_ref[:, :, :DELTA_WIDTH].reshape(tc, DELTA_WIDTH)
    k_raw = dl_ref[:, :, DELTA_WIDTH:2 * DELTA_WIDTH].reshape(tc, DELTA_WIDTH)
    v_all = dl_ref[:, :, 2 * DELTA_WIDTH:qkv_w].reshape(tc, DELTA_WIDTH)

    r512 = _head_of(lax.broadcasted_iota(jnp.int32, (DELTA_WIDTH, DELTA_WIDTH), 0))
    c512 = _head_of(lax.broadcasted_iota(jnp.int32, (DELTA_WIDTH, DELTA_WIDTH), 1))
    head_ones = (r512 == c512).astype(BF16)

    def head_sum(x):
        return _dot(x.astype(BF16), head_ones)

    q_all = q_raw * lax.rsqrt(head_sum(q_raw * q_raw) + EPS) * (HEAD_DIM ** -0.5)
    k_all = k_raw * lax.rsqrt(head_sum(k_raw * k_raw) + EPS)

    t = ba_ref[...].reshape(tc, LANES)
    za = t + dtb_ref[...]
    softplus = jnp.maximum(za, 0.0) + jnp.log1p(jnp.exp(-jnp.abs(za)))
    g_lane = -jnp.exp(alog_ref[...]) * softplus
    er = lax.broadcasted_iota(jnp.int32, (LANES, DELTA_WIDTH), 0)
    ec = _head_of(lax.broadcasted_iota(jnp.int32, (LANES, DELTA_WIDTH), 1))
    expand_beta = (er == ec).astype(BF16)
    expand_g = (er == ec + N_DELTA_HEADS).astype(BF16)
    beta_all = _dot(_sigmoid(t).astype(BF16), expand_beta)
    g_all = jnp.zeros((tc, DELTA_WIDTH), F32)
    for part in _split3(g_lane):
        g_all = g_all + _dot(part, expand_g)

    ci = lax.broadcasted_iota(jnp.int32, (CHUNK, CHUNK), 0)
    cj = lax.broadcasted_iota(jnp.int32, (CHUNK, CHUNK), 1)
    lower_incl = (ci >= cj).astype(BF16)
    all_ones = jnp.ones((CHUNK, CHUNK), BF16)
    row_w = lax.broadcasted_iota(jnp.int32, (CHUNK, DELTA_WIDTH), 0)
    col_w = _within_head(lax.broadcasted_iota(jnp.int32, (CHUNK, DELTA_WIDTH), 1))
    upper_incl_w = row_w <= col_w
    row_g = lax.broadcasted_iota(jnp.int32, (CHUNK, GROUP_W), 0)
    col_g = _within_head(lax.broadcasted_iota(jnp.int32, (CHUNK, GROUP_W), 1))
    causal_g = row_g >= col_g
    strict_g = row_g > col_g
    eye_g = (row_g == col_g).astype(F32)
    level_masks = []
    for n in range(int(math.log2(CHUNK))):
        shr = lambda v, k: lax.shift_right_logical(v, k)
        level_masks.append((shr(row_g, n + 1) == shr(col_g, n + 1)) & (shr(row_g, n) > shr(col_g, n)))
    sr = _head_of(lax.broadcasted_iota(jnp.int32, (GROUP_W, GROUP_W), 0))
    sc = _head_of(lax.broadcasted_iota(jnp.int32, (GROUP_W, GROUP_W), 1))
    state_mask = sr == sc

    chains = []
    for c in range(bsz):
        rows = slice(c * CHUNK, (c + 1) * CHUNK)
        gb = g_all[rows]
        gcum = jnp.zeros((CHUNK, DELTA_WIDTH), F32)
        gcum_t = jnp.zeros((CHUNK, DELTA_WIDTH), F32)
        for part in _split3(gb):
            gcum = gcum + _dot(lower_incl, part)
        for part in _split3(jnp.where(upper_incl_w, gb, 0.0)):
            gcum_t = gcum_t + _dot(all_ones, part)
        g_last = gcum[CHUNK - 1:CHUNK, :]
        exp_g = jnp.exp(gcum)
        exp_dec = jnp.exp(g_last - gcum)
        exp_last = jnp.exp(g_last)

        kc, qc, vc, bb = k_all[rows], q_all[rows], v_all[rows], beta_all[rows]
        kb = kc * bb
        vb = vc * bb
        kbe = kb * exp_g
        qe = qc * exp_g
        kdec = kc * exp_dec
        for g in range(n_groups):
            gl = slice(g * GROUP_W, (g + 1) * GROUP_W)
            diff = gcum[:, gl] - gcum_t[:, gl]
            decay = jnp.where(causal_g, jnp.exp(jnp.where(causal_g, diff, 0.0)), 0.0)
            aq = _dot_nt(jnp.concatenate([kb[:, gl], qc[:, gl]], axis=0).astype(BF16), _bd_stack(kc[:, gl]))
            chains.append(dict(
                slot=c * n_groups + g, a=jnp.where(strict_g, aq[:CHUNK] * decay, 0.0),
                qk=(aq[CHUNK:] * decay).astype(BF16), qe=qe[:, gl], kdec_t=kdec[:, gl].T.astype(BF16),
                exp_last=exp_last[:, gl],
                rhs=jnp.concatenate([_bd_stack(vb[:, gl]), _bd_stack(kbe[:, gl])], axis=1)))

    t_invs = [eye_g - jnp.where(level_masks[0], ch["a"], 0.0) for ch in chains]
    for mask in level_masks[1:]:
        ys = [_dot(jnp.where(mask, ch["a"], 0.0).astype(BF16), _bd_stack(t)) for ch, t in zip(chains, t_invs)]
        t_invs = [t - _dot(t.astype(BF16), _bd_stack(y)) for t, y in zip(t_invs, ys)]
    uws = [_dot(t.astype(BF16), ch["rhs"]) for ch, t in zip(chains, t_invs)]

    states = [state_ref[ch["slot"]] for ch in chains]
    wss = [_dot(jnp.concatenate([uw[:, GROUP_W:], ch["qe"]], axis=0).astype(BF16), st.astype(BF16))
           for ch, uw, st in zip(chains, uws, states)]
    v_news = [uw[:, :GROUP_W] - ws[:CHUNK] for uw, ws in zip(uws, wss)]
    outs = [ws[CHUNK:] + _dot(ch["qk"], _bd_stack(vn)) for ch, ws, vn in zip(chains, wss, v_news)]
    upds = [_dot(ch["kdec_t"], vn.astype(BF16)) for ch, vn in zip(chains, v_news)]
    for ch, st, upd in zip(chains, states, upds):
        state_ref[ch["slot"]] = st * ch["exp_last"] + jnp.where(state_mask, upd, 0.0)

    o_all = jnp.concatenate(
        [jnp.concatenate(outs[c * n_groups:(c + 1) * n_groups], axis=1) for c in range(bsz)], axis=0)
    ms = head_sum(o_all * o_all) * (1.0 / HEAD_DIM)
    z = dl_ref[:, :, qkv_w:].reshape(tc, DELTA_WIDTH)
    o_ref[...] = (o_all * lax.rsqrt(ms + EPS) * ng_ref[...] * _silu(z)).reshape(bsz, CHUNK, DELTA_WIDTH)


def _delta_call(dl, ba, alog_row, dtb_row, ng_row):
    bsz, s, wd = dl.shape
    n_groups = DELTA_WIDTH // GROUP_W
    chunk_spec = lambda w: pl.BlockSpec((bsz, CHUNK, w), lambda i: (0, i, 0))
    return pl.pallas_call(
        _delta_kernel,
        grid=(s // CHUNK,),
        in_specs=[
            chunk_spec(wd),
            chunk_spec(LANES),
            _const_spec((1, LANES)),
            _const_spec((1, LANES)),
            _const_spec((1, DELTA_WIDTH)),
        ],
        out_specs=chunk_spec(DELTA_WIDTH),
        out_shape=jax.ShapeDtypeStruct((bsz, s, DELTA_WIDTH), F32),
        scratch_shapes=[pltpu.VMEM((bsz * n_groups, GROUP_W, GROUP_W), F32)],
        compiler_params=_compiler_params(("arbitrary",)),
        name="delta",
    )(dl, ba, alog_row, dtb_row, ng_row)


def _out_ffn_kernel(x_ref, o1_ref, o2_ref, o3_ref, l1_ref, l2_ref, l3_ref, yd_ref, mod_ref,
                    wtop_ref, wbot_ref, g2_ref, wg_ref, wu_ref, wdn_ref, gf_ref, out_ref, natural_ref, *, final_norm):
    tm = x_ref.shape[1]
    groups = [slice(r, r + tm // FFN_ROW_GROUPS) for r in range(0, tm, tm // FFN_ROW_GROUPS)]

    staged = 0

    def natural_rows(ref):
        nonlocal staged
        n_classes = ref.shape[1]
        if n_classes == 1:
            return lambda rows: ref[0, 0, rows, :]
        slot = staged
        staged += 1
        n_slabs = ref.shape[3] // LANES
        for sl in range(n_slabs):
            for cls in range(n_classes):
                natural_ref[slot, sl, pl.ds(cls, tm // n_classes, stride=n_classes), :] = (
                    ref[0, cls, :, sl * LANES:(sl + 1) * LANES])
        return lambda rows: jnp.concatenate([natural_ref[slot, sl, rows, :] for sl in range(n_slabs)], axis=1)

    o_rows = [natural_rows(r) for r in (o1_ref, o2_ref, o3_ref)]
    l_rows = [natural_rows(r) for r in (l1_ref, l2_ref, l3_ref)]

    def merge(rows):
        lses = [get(rows) for get in l_rows]
        outs = [get(rows) for get in o_rows]
        mx = jnp.maximum(jnp.maximum(lses[0], lses[1]), lses[2])
        ws = [jnp.exp(l - mx) for l in lses]
        den = ws[0] + ws[1] + ws[2]
        return ((ws[0] * outs[0] + ws[1] * outs[1] + ws[2] * outs[2]) / den).astype(BF16)

    ya = [merge(rows) for rows in groups]
    y = [_dot(a, wtop_ref[...]) + _dot(yd_ref[0, rows, :].astype(BF16), wbot_ref[...]) for a, rows in zip(ya, groups)]
    x1 = [x_ref[0, rows, :] + mod_ref[0, 2:3, :] * yy for yy, rows in zip(y, groups)]
    hb = [(_rmsnorm_rows(xx, g2_ref[...]) * (1.0 + mod_ref[0, 4:5, :]) + mod_ref[0, 3:4, :]).astype(BF16) for xx in x1]
    gate = [_dot(h, wg_ref[...]) for h in hb]
    up = [_dot(h, wu_ref[...]) for h in hb]
    y2 = [_dot((_silu(g) * u).astype(BF16), wdn_ref[...]) for g, u in zip(gate, up)]
    for rows, xx, yy in zip(groups, x1, y2):
        x2 = xx + mod_ref[0, 5:6, :] * yy
        out_ref[0, rows, :] = _rmsnorm_rows(x2, gf_ref[...]) if final_norm else x2


def _out_ffn_call(x, o_list, lse_list, yd, mod3, wtop, wbot, g2, wg, wu, wdn, gf, final_norm):
    bsz, s, d = x.shape
    tm = FFN_TOKEN_TILE
    aw = o_list[0].shape[-1]
    dff = wg.shape[1]
    row_spec = lambda w: pl.BlockSpec((1, tm, w), lambda b, i: (b, i, 0))
    branch_spec = lambda t: pl.BlockSpec((1, t.shape[1], tm // t.shape[1], aw), lambda b, i: (b, 0, i, 0))
    n_staged = sum(t.shape[1] > 1 for t in o_list + lse_list)
    return pl.pallas_call(
        functools.partial(_out_ffn_kernel, final_norm=final_norm),
        grid=(bsz, s // tm),
        in_specs=[row_spec(d)] + [branch_spec(t) for t in o_list + lse_list] + [row_spec(yd.shape[-1])] + [
            pl.BlockSpec((1, 6, d), lambda b, i: (b, 0, 0)),
            _const_spec(wtop.shape), _const_spec(wbot.shape), _const_spec((1, d)),
            _const_spec((d, dff)), _const_spec((d, dff)), _const_spec((dff, d)), _const_spec((1, d)),
        ],
        out_specs=row_spec(d),
        out_shape=jax.ShapeDtypeStruct((bsz, s, d), x.dtype),
        scratch_shapes=[pltpu.VMEM((max(n_staged, 1), aw // LANES, tm, LANES), F32)],
        compiler_params=_compiler_params(("parallel", "parallel")),
        name="out_ffn",
    )(x, *o_list, *lse_list, yd, mod3, wtop, wbot, g2, wg, wu, wdn, gf)


def kernel(x, c, w_ada, b_ada, norm_attn_g, w_in, rel_bias, conv_w, a_log, dt_bias, delta_norm_g, w_out,
           norm_ffn_g, w_gate, w_up, w_down, final_norm_g):
    bsz, s, d = x.shape
    depth = w_ada.shape[0]
    assert s % PAD_UNIT == 0 and s % CHUNK == 0 and s % TOKEN_TILE == 0
    assert all(window // dilation == BAND for window, dilation in DILATED_BRANCHES)

    bias = _bias_call(_bucket_index_tables(), rel_bias)
    lane_pad = LANES - 2 * N_DELTA_HEADS
    for l in range(depth):
        mod3 = _mod_call(c, w_ada[l], b_ada[l]).reshape(bsz, 6, d)
        qkv_w = 3 * ATTN_WIDTH
        dl_w = 4 * DELTA_WIDTH
        conv_cols = 3 * DELTA_WIDTH
        wa = w_in[l][:, :qkv_w].astype(BF16)
        wc = w_in[l][:, qkv_w:qkv_w + conv_cols].astype(BF16)
        wz = w_in[l][:, qkv_w + conv_cols:qkv_w + dl_w].astype(BF16)
        wba = jnp.pad(w_in[l][:, qkv_w + dl_w:], ((0, 0), (0, lane_pad))).astype(BF16)
        *qkv_by_branch, dl, ba = _inproj_call(
            x, mod3, norm_attn_g[l].reshape(1, d), wa, wc, wz, wba, conv_w[l].astype(F32))

        o_list, lse_list = [], []
        for branch, ((_, dilation), qkv) in enumerate(zip(DILATED_BRANCHES, qkv_by_branch)):
            out_classes = max(dilation // MAX_FAST_ROW_STRIDE, 1)
            o, lse = _attn_call(qkv.reshape(bsz, qkv.shape[1], s, LANES), bias, branch, dilation, out_classes)
            o_list.append(o)
            lse_list.append(lse)

        gate_pad = lambda v: jnp.pad(v.astype(F32), (N_DELTA_HEADS, LANES - 2 * N_DELTA_HEADS)).reshape(1, LANES)
        yd = _delta_call(dl, ba, gate_pad(a_log[l]), gate_pad(dt_bias[l]),
                         jnp.tile(delta_norm_g[l].astype(F32), N_DELTA_HEADS).reshape(1, DELTA_WIDTH))

        wo = w_out[l].astype(BF16)
        x = _out_ffn_call(
            x, o_list, lse_list, yd, mod3, wo[:ATTN_WIDTH], wo[ATTN_WIDTH:], norm_ffn_g[l].reshape(1, d),
            w_gate[l].astype(BF16), w_up[l].astype(BF16), w_down[l].astype(BF16), final_norm_g.reshape(1, d),
            final_norm=(l == depth - 1))
    return x
```

```python
import functools
import math

import jax
import jax.numpy as jnp
import numpy as np
from jax import lax
from jax.experimental import pallas as pl
from jax.experimental.pallas import tpu as pltpu

F32 = jnp.float32
BF16 = jnp.bfloat16

HEAD_DIM = 64
N_ATTN_HEADS = 8
N_DELTA_HEADS = 8
ATTN_WIDTH = N_ATTN_HEADS * HEAD_DIM
DELTA_WIDTH = N_DELTA_HEADS * HEAD_DIM
DILATED_BRANCHES = ((128, 1), (512, 4), (2048, 16))
BAND = 128
D4, D16 = DILATED_BRANCHES[1][1], DILATED_BRANCHES[2][1]
PAD_UNIT = 2048
N_BUCKETS = 32
MAX_DISTANCE = 2048
CONV_WIDTH = 4
CHUNK = 64
EPS = 1e-6
NEG_INF = -1e30

LANES = 128
MXU_COLS = 256
MAX_FAST_ROW_STRIDE = 4
VMEM_LIMIT_BYTES = 60000 * 1024

TOKEN_TILE = 512
HALO = 16
FFN_TOKEN_TILE = 512
FFN_ROW_GROUPS = 2
ATTN_UNROLL = 8
HEADS_PER_GROUP = 4
GROUP_W = HEADS_PER_GROUP * HEAD_DIM


def _dot(a, b):
    return jnp.dot(a, b, preferred_element_type=F32)


def _dot_nt(a, b):
    return lax.dot_general(a, b, (((1,), (1,)), ((), ())), preferred_element_type=F32)


def _split3(x):
    x1 = x.astype(BF16)
    r1 = x - x1.astype(F32)
    x2 = r1.astype(BF16)
    x3 = (r1 - x2.astype(F32)).astype(BF16)
    return x1, x2, x3


def _split2(x):
    x1 = x.astype(BF16)
    x2 = (x - x1.astype(F32)).astype(BF16)
    return x1, x2


def _sigmoid(x):
    return 1.0 / (1.0 + jnp.exp(-x))


def _silu(x):
    return x * _sigmoid(x)


def _head_of(idx):
    return lax.shift_right_logical(idx, int(math.log2(HEAD_DIM)))


def _within_head(idx):
    return lax.bitwise_and(idx, HEAD_DIM - 1)


def _compiler_params(semantics):
    return pltpu.CompilerParams(dimension_semantics=semantics, vmem_limit_bytes=VMEM_LIMIT_BYTES)


def _const_spec(shape):
    zeros = (0,) * len(shape)
    return pl.BlockSpec(shape, lambda *_: zeros, pipeline_mode=pl.Buffered(1))


def _mod_kernel(c_ref, w_ref, b_ref, o_ref):
    ca = _silu(c_ref[...])
    acc = b_ref[...]
    w = w_ref[...]
    for part in _split2(ca):
        for wpart in _split2(w):
            acc = acc + _dot(part, wpart)
    o_ref[...] = acc


def _mod_call(c, w_ada, b_ada):
    bsz, d = c.shape
    n = w_ada.shape[1]
    blk = d
    return pl.pallas_call(
        _mod_kernel,
        grid=(n // blk,),
        in_specs=[
            pl.BlockSpec((bsz, d), lambda j: (0, 0)),
            pl.BlockSpec((d, blk), lambda j: (0, j)),
            pl.BlockSpec((1, blk), lambda j: (0, j)),
        ],
        out_specs=pl.BlockSpec((bsz, blk), lambda j: (0, j)),
        out_shape=jax.ShapeDtypeStruct((bsz, n), F32),
        compiler_params=_compiler_params(("arbitrary",)),
        name="mod",
    )(c, w_ada, b_ada.reshape(1, n))


def _rmsnorm_rows(x, g):
    return x * lax.rsqrt(jnp.mean(x * x, axis=-1, keepdims=True) + EPS) * g


def _inproj_kernel(x_ref, xh_ref, mod_ref, g_ref, wa_ref, wc_ref, wz_ref, wba_ref, cw_ref,
                   qkv1_ref, qkv4_ref, qkv16_ref, dl_ref, ba_ref, hb_ref, *block_scratch):
    tm = x_ref.shape[1]
    conv_w = wc_ref.shape[1]
    n_attn_blocks = wa_ref.shape[1] // MXU_COLS
    stage_refs, pre_refs = block_scratch[:2 * n_attn_blocks], block_scratch[2 * n_attn_blocks:]
    xe = jnp.concatenate([xh_ref[0], x_ref[0]], axis=0)
    h = _rmsnorm_rows(xe, g_ref[...]) * (1.0 + mod_ref[0, 1:2, :]) + mod_ref[0, 0:1, :]
    hb_ref[...] = h.astype(BF16)

    n_slabs = wa_ref.shape[1] // LANES

    def attn_block(j0):
        blk = _dot(hb_ref[HALO:, :], wa_ref[:, j0 * LANES:j0 * LANES + MXU_COLS])
        stage_ref, stage4_ref = stage_refs[2 * (j0 * LANES // MXU_COLS):][:2]
        for j in range(j0, j0 + MXU_COLS // LANES):
            jj = j - j0
            slab = blk[:, jj * LANES:(jj + 1) * LANES]
            stage_ref[jj] = slab
            qkv1_ref[0, j] = slab.astype(BF16)
            for r in range(D4):
                cls = stage_ref[jj, pl.ds(r, tm // D4, stride=D4), :]
                stage4_ref[jj, r] = cls
                qkv4_ref[0, j, r] = cls.astype(BF16)
            for r in range(D16):
                sub = stage4_ref[jj, r % D4, pl.ds(r // D4, tm // D16, stride=D16 // D4), :]
                qkv16_ref[0, j, r] = sub.astype(BF16)

    row = lax.broadcasted_iota(jnp.int32, (tm + HALO, MXU_COLS), 0)
    no_history = (row < HALO) & (pl.program_id(1) == 0)

    def conv_block(c0):
        cols = slice(c0, c0 + MXU_COLS)
        pre = jnp.where(no_history, 0.0, _dot(hb_ref[...], wc_ref[:, cols]))
        pre_ref, out_stage_ref = pre_refs[2 * (c0 // MXU_COLS):][:2]
        for sl in range(MXU_COLS // LANES):
            lanes = slice(sl * LANES, (sl + 1) * LANES)
            pre_ref[sl] = pre[:, lanes]
            taps = [cw_ref[k:k + 1, c0 + sl * LANES:c0 + (sl + 1) * LANES] for k in range(CONV_WIDTH)]
            for parity in range(2):
                conv = None
                for k in range(CONV_WIDTH):
                    start = HALO + parity - (CONV_WIDTH - 1) + k
                    term = pre_ref[sl, pl.ds(start, tm // 2, stride=2), :] * taps[k]
                    conv = term if conv is None else conv + term
                out_stage_ref[sl, pl.ds(parity, tm // 2, stride=2), :] = _silu(conv)
            dl_ref[0, :, c0 + sl * LANES:c0 + (sl + 1) * LANES] = out_stage_ref[sl]

    attn_starts = list(range(0, n_slabs, MXU_COLS // LANES))
    conv_starts = list(range(0, conv_w, MXU_COLS))
    while attn_starts or conv_starts:
        if conv_starts:
            conv_block(conv_starts.pop(0))
        if attn_starts:
            attn_block(attn_starts.pop(0))
    for c0 in range(0, wz_ref.shape[1], MXU_COLS):
        dl_ref[0, :, conv_w + c0:conv_w + c0 + MXU_COLS] = _dot(hb_ref[HALO:, :], wz_ref[:, c0:c0 + MXU_COLS])
    ba_ref[0] = _dot(hb_ref[HALO:, :], wba_ref[...])


def _inproj_call(x, mod3, g, wa, wc, wz, wba, conv_w):
    bsz, s, d = x.shape
    tm = TOKEN_TILE
    na, nc, nz, nba = wa.shape[1], wc.shape[1], wz.shape[1], wba.shape[1]
    n_slabs = na // LANES
    d4, d16 = DILATED_BRANCHES[1][1], DILATED_BRANCHES[2][1]
    row_spec = lambda w: pl.BlockSpec((1, tm, w), lambda b, i: (b, i, 0))
    cm_spec = lambda dil: pl.BlockSpec((1, n_slabs, dil, tm // dil, LANES), lambda b, i: (b, 0, 0, i, 0))
    cm_shape = lambda dil: jax.ShapeDtypeStruct((bsz, n_slabs, dil, s // dil, LANES), BF16)
    return pl.pallas_call(
        _inproj_kernel,
        grid=(bsz, s // tm),
        in_specs=[
            row_spec(d),
            pl.BlockSpec((1, HALO, d), lambda b, i: (b, jnp.maximum(i * (tm // HALO) - 1, 0), 0)),
            pl.BlockSpec((1, 6, d), lambda b, i: (b, 0, 0)),
            _const_spec((1, d)),
            _const_spec((d, na)),
            _const_spec((d, nc)),
            _const_spec((d, nz)),
            _const_spec((d, nba)),
            _const_spec(conv_w.shape),
        ],
        out_specs=[
            pl.BlockSpec((1, n_slabs, tm, LANES), lambda b, i: (b, 0, i, 0)),
            cm_spec(d4),
            cm_spec(d16),
            row_spec(nc + nz),
            row_spec(nba),
        ],
        out_shape=[
            jax.ShapeDtypeStruct((bsz, n_slabs, s, LANES), BF16),
            cm_shape(d4),
            cm_shape(d16),
            jax.ShapeDtypeStruct((bsz, s, nc + nz), F32),
            jax.ShapeDtypeStruct((bsz, s, nba), F32),
        ],
        scratch_shapes=(
            [pltpu.VMEM((tm + HALO, d), BF16)]
            + [pltpu.VMEM((MXU_COLS // LANES, tm, LANES), F32),
               pltpu.VMEM((MXU_COLS // LANES, D4, tm // D4, LANES), F32)] * (na // MXU_COLS)
            + [pltpu.VMEM((MXU_COLS // LANES, tm + HALO, LANES), F32),
               pltpu.VMEM((MXU_COLS // LANES, tm, LANES), F32)] * (nc // MXU_COLS)
        ),
        compiler_params=_compiler_params(("parallel", "parallel")),
        name="inproj",
    )(x, x, mod3, g, wa, wc, wz, wba, conv_w)


def _t5_bucket(distance):
    max_exact = N_BUCKETS // 2
    dist_f = jnp.maximum(distance, 1).astype(F32)
    large = max_exact + (jnp.log(dist_f / max_exact) / math.log(MAX_DISTANCE / max_exact)
                         * (N_BUCKETS - max_exact)).astype(jnp.int32)
    return jnp.where(distance < max_exact, distance, jnp.minimum(large, N_BUCKETS - 1))


def _bucket_index_tables():
    qi = jnp.arange(BAND)[:, None]
    kj = jnp.arange(2 * BAND)[None, :]
    steps = qi + BAND - kj
    in_window = (steps >= 0) & (steps <= BAND)
    tables = [jnp.where(in_window, _t5_bucket(jnp.maximum(steps, 0) * dilation), -1)
              for _, dilation in DILATED_BRANCHES]
    return jnp.stack(tables).astype(jnp.int32)


def _bias_kernel(idx_ref, rb_ref, o_ref):
    idx = idx_ref[0]
    in_current = lax.broadcasted_iota(jnp.int32, idx.shape, 1) >= BAND
    masked_half = jnp.full((BAND, BAND), NEG_INF, F32)
    for h in range(N_ATTN_HEADS):
        acc = jnp.full(idx.shape, NEG_INF, F32)
        for b in range(N_BUCKETS):
            acc = jnp.where(idx == b, rb_ref[b, h], acc)
        o_ref[0, h, 0] = acc
        o_ref[0, h, 1] = jnp.where(in_current, acc, NEG_INF)
        o_ref[0, h, 2] = jnp.concatenate([acc[:, BAND:], masked_half], axis=1)


def _bias_call(idx_tables, rel_bias):
    nbr = idx_tables.shape[0]
    blk = (3,) + idx_tables.shape[1:]
    return pl.pallas_call(
        _bias_kernel,
        grid=(nbr,),
        in_specs=[
            pl.BlockSpec((1,) + idx_tables.shape[1:], lambda i: (i, 0, 0)),
            pl.BlockSpec(memory_space=pltpu.SMEM),
        ],
        out_specs=pl.BlockSpec((1, N_ATTN_HEADS) + blk, lambda i: (i, 0, 0, 0, 0)),
        out_shape=jax.ShapeDtypeStruct((nbr, N_ATTN_HEADS) + blk, F32),
        compiler_params=_compiler_params(("arbitrary",)),
        name="bias",
    )(idx_tables, rel_bias.astype(F32))


def _attn_kernel(q_ref, k_ref, v_ref, bias_ref, o_ref, lse_ref, *, dilation, out_classes, blocks_per_class,
                 n_blocks):
    lane = lax.broadcasted_iota(jnp.int32, (BAND, LANES), 1)
    is_a = lane < HEAD_DIM
    class_shift = int(math.log2(blocks_per_class))
    out_stride = dilation // out_classes

    def body(jg, carry):
        blocks = []
        for u in range(ATTN_UNROLL):
            j = jg * ATTN_UNROLL + u
            r0 = pl.multiple_of(j * BAND, BAND)
            ks = pl.multiple_of(jnp.maximum(j - 1, 0) * BAND, BAND)
            n_in_class = lax.bitwise_and(j, blocks_per_class - 1)
            variant = jnp.where(n_in_class == 0, jnp.where(j > 0, 1, 2), 0)
            q = q_ref[0, 0, pl.ds(r0, BAND), :] * (HEAD_DIM ** -0.5)
            res_class = lax.shift_right_logical(j, class_shift)
            blocks.append(dict(
                q=q, kw=k_ref[0, 0, pl.ds(ks, 2 * BAND), :], vw=v_ref[0, 0, pl.ds(ks, 2 * BAND), :], variant=variant,
                out_class=lax.bitwise_and(res_class, out_classes - 1),
                row0=lax.shift_right_logical(res_class, int(math.log2(out_classes))) + out_stride * BAND * n_in_class))
        chains = [(blk, hh) for blk in blocks for hh in range(2)]
        zero = jnp.zeros((BAND, LANES), BF16)
        s_all = [_dot_nt(jnp.where(is_a if hh == 0 else jnp.logical_not(is_a), blk["q"], zero), blk["kw"])
                 + bias_ref[0, hh, blk["variant"]] for blk, hh in chains]
        m_all = [jnp.max(s, axis=-1, keepdims=True) for s in s_all]
        e_all = [jnp.exp(s - m) for s, m in zip(s_all, m_all)]
        d_all = [jnp.sum(e, axis=-1, keepdims=True) for e in e_all]
        o_all = [_dot(e.astype(BF16), blk["vw"]) / d for e, d, (blk, _) in zip(e_all, d_all, chains)]
        lse_all = [jnp.broadcast_to(m + jnp.log(d), (BAND, LANES)) for m, d in zip(m_all, d_all)]
        for u, blk in enumerate(blocks):
            rows = pl.ds(blk["row0"], BAND, stride=out_stride)
            o_ref[0, blk["out_class"], rows, :] = jnp.where(is_a, o_all[2 * u], o_all[2 * u + 1])
            lse_ref[0, blk["out_class"], rows, :] = jnp.where(is_a, lse_all[2 * u], lse_all[2 * u + 1])
        return carry

    lax.fori_loop(0, n_blocks // ATTN_UNROLL, body, 0)


def _attn_call(qkv, bias, branch, dilation, out_classes):
    bsz, n_slabs, s, _ = qkv.shape
    hp = n_slabs // 3
    n_blocks = s // BAND
    blocks_per_class = n_blocks // dilation
    assert blocks_per_class & (blocks_per_class - 1) == 0 and n_blocks % ATTN_UNROLL == 0
    assert dilation % out_classes == 0 and out_classes & (out_classes - 1) == 0
    kern = functools.partial(_attn_kernel, dilation=dilation, out_classes=out_classes,
                             blocks_per_class=blocks_per_class, n_blocks=n_blocks)
    slab_spec = lambda t: pl.BlockSpec((1, 1, s, LANES), lambda h, b: (b, t * hp + h, 0, 0))
    out_spec = pl.BlockSpec((1, out_classes, s // out_classes, LANES), lambda h, b: (b, 0, 0, h))
    return pl.pallas_call(
        kern,
        grid=(hp, bsz),
        in_specs=[
            slab_spec(0), slab_spec(1), slab_spec(2),
            pl.BlockSpec((1, 2, 3, BAND, 2 * BAND), lambda h, b: (branch, h, 0, 0, 0)),
        ],
        out_specs=[out_spec, out_spec],
        out_shape=[jax.ShapeDtypeStruct((bsz, out_classes, s // out_classes, hp * LANES), F32)] * 2,
        compiler_params=_compiler_params(("parallel", "parallel")),
        name=f"attn_d{dilation}",
    )(qkv, qkv, qkv, bias)


def _bd_stack(x):
    xb = x.astype(BF16)
    head_of_lane = _head_of(lax.broadcasted_iota(jnp.int32, xb.shape, 1))
    zero = jnp.zeros_like(xb)
    return jnp.concatenate([jnp.where(head_of_lane == h, xb, zero) for h in range(HEADS_PER_GROUP)], axis=0)


def _delta_kernel(dl_ref, ba_ref, alog_ref, dtb_ref, ng_ref, o_ref, state_ref):
    bsz = dl_ref.shape[0]
    tc = bsz * CHUNK
    qkv_w = 3 * DELTA_WIDTH
    n_groups = DELTA_WIDTH // GROUP_W

    @pl.when(pl.program_id(0) == 0)
    def _():
        state_ref[...] = jnp.zeros_like(state_ref)

    q_raw = dl_ref[:, :, :DELTA_WIDTH].reshape(tc, DELTA_WIDTH)
    k_raw = dl_ref[:, :, DELTA_WIDTH:2 * DELTA_WIDTH].reshape(tc, DELTA_WIDTH)
    v_all = dl_ref[:, :, 2 * DELTA_WIDTH:qkv_w].reshape(tc, DELTA_WIDTH)

    r512 = _head_of(lax.broadcasted_iota(jnp.int32, (DELTA_WIDTH, DELTA_WIDTH), 0))
    c512 = _head_of(lax.broadcasted_iota(jnp.int32, (DELTA_WIDTH, DELTA_WIDTH), 1))
    head_ones = (r512 == c512).astype(BF16)

    def head_sum(x):
        return _dot(x.astype(BF16), head_ones)

    q_all = q_raw * lax.rsqrt(head_sum(q_raw * q_raw) + EPS) * (HEAD_DIM ** -0.5)
    k_all = k_raw * lax.rsqrt(head_sum(k_raw * k_raw) + EPS)

    t = ba_ref[...].reshape(tc, LANES)
    za = t + dtb_ref[...]
    softplus = jnp.maximum(za, 0.0) + jnp.log1p(jnp.exp(-jnp.abs(za)))
    g_lane = -jnp.exp(alog_ref[...]) * softplus
    er = lax.broadcasted_iota(jnp.int32, (LANES, DELTA_WIDTH), 0)
    ec = _head_of(lax.broadcasted_iota(jnp.int32, (LANES, DELTA_WIDTH), 1))
    expand_beta = (er == ec).astype(BF16)
    expand_g = (er == ec + N_DELTA_HEADS).astype(BF16)
    beta_all = _dot(_sigmoid(t).astype(BF16), expand_beta)
    g_all = jnp.zeros((tc, DELTA_WIDTH), F32)
    for part in _split3(g_lane):
        g_all = g_all + _dot(part, expand_g)

    ci = lax.broadcasted_iota(jnp.int32, (CHUNK, CHUNK), 0)
    cj = lax.broadcasted_iota(jnp.int32, (CHUNK, CHUNK), 1)
    lower_incl = (ci >= cj).astype(BF16)
    all_ones = jnp.ones((CHUNK, CHUNK), BF16)
    row_w = lax.broadcasted_iota(jnp.int32, (CHUNK, DELTA_WIDTH), 0)
    col_w = _within_head(lax.broadcasted_iota(jnp.int32, (CHUNK, DELTA_WIDTH), 1))
    upper_incl_w = row_w <= col_w
    row_g = lax.broadcasted_iota(jnp.int32, (CHUNK, GROUP_W), 0)
    col_g = _within_head(lax.broadcasted_iota(jnp.int32, (CHUNK, GROUP_W), 1))
    causal_g = row_g >= col_g
    strict_g = row_g > col_g
    eye_g = (row_g == col_g).astype(F32)
    level_masks = []
    for n in range(int(math.log2(CHUNK))):
        shr = lambda v, k: lax.shift_right_logical(v, k)
        level_masks.append((shr(row_g, n + 1) == shr(col_g, n + 1)) & (shr(row_g, n) > shr(col_g, n)))
    sr = _head_of(lax.broadcasted_iota(jnp.int32, (GROUP_W, GROUP_W), 0))
    sc = _head_of(lax.broadcasted_iota(jnp.int32, (GROUP_W, GROUP_W), 1))
    state_mask = sr == sc

    chains = []
    for c in range(bsz):
        rows = slice(c * CHUNK, (c + 1) * CHUNK)
        gb = g_all[rows]
        gcum = jnp.zeros((CHUNK, DELTA_WIDTH), F32)
        gcum_t = jnp.zeros((CHUNK, DELTA_WIDTH), F32)
        for part in _split3(gb):
            gcum = gcum + _dot(lower_incl, part)
        for part in _split3(jnp.where(upper_incl_w, gb, 0.0)):
            gcum_t = gcum_t + _dot(all_ones, part)
        g_last = gcum[CHUNK - 1:CHUNK, :]
        exp_g = jnp.exp(gcum)
        exp_dec = jnp.exp(g_last - gcum)
        exp_last = jnp.exp(g_last)

        kc, qc, vc, bb = k_all[rows], q_all[rows], v_all[rows], beta_all[rows]
        kb = kc * bb
        vb = vc * bb
        kbe = kb * exp_g
        qe = qc * exp_g
        kdec = kc * exp_dec
        for g in range(n_groups):
            gl = slice(g * GROUP_W, (g + 1) * GROUP_W)
            diff = gcum[:, gl] - gcum_t[:, gl]
            decay = jnp.where(causal_g, jnp.exp(jnp.where(causal_g, diff, 0.0)), 0.0)
            aq = _dot_nt(jnp.concatenate([kb[:, gl], qc[:, gl]], axis=0).astype(BF16), _bd_stack(kc[:, gl]))
            chains.append(dict(
                slot=c * n_groups + g, a=jnp.where(strict_g, aq[:CHUNK] * decay, 0.0),
                qk=(aq[CHUNK:] * decay).astype(BF16), qe=qe[:, gl], kdec_t=kdec[:, gl].T.astype(BF16),
                exp_last=exp_last[:, gl],
                rhs=jnp.concatenate([_bd_stack(vb[:, gl]), _bd_stack(kbe[:, gl])], axis=1)))

    t_invs = [eye_g - jnp.where(level_masks[0], ch["a"], 0.0) for ch in chains]
    for mask in level_masks[1:]:
        ys = [_dot(jnp.where(mask, ch["a"], 0.0).astype(BF16), _bd_stack(t)) for ch, t in zip(chains, t_invs)]
        t_invs = [t - _dot(t.astype(BF16), _bd_stack(y)) for t, y in zip(t_invs, ys)]
    uws = [_dot(t.astype(BF16), ch["rhs"]) for ch, t in zip(chains, t_invs)]

    states = [state_ref[ch["slot"]] for ch in chains]
    wss = [_dot(jnp.concatenate([uw[:, GROUP_W:], ch["qe"]], axis=0).astype(BF16), st.astype(BF16))
           for ch, uw, st in zip(chains, uws, states)]
    v_news = [uw[:, :GROUP_W] - ws[:CHUNK] for uw, ws in zip(uws, wss)]
    outs = [ws[CHUNK:] + _dot(ch["qk"], _bd_stack(vn)) for ch, ws, vn in zip(chains, wss, v_news)]
    upds = [_dot(ch["kdec_t"], vn.astype(BF16)) for ch, vn in zip(chains, v_news)]
    for ch, st, upd in zip(chains, states, upds):
        state_ref[ch["slot"]] = st * ch["exp_last"] + jnp.where(state_mask, upd, 0.0)

    o_all = jnp.concatenate(
        [jnp.concatenate(outs[c * n_groups:(c + 1) * n_groups], axis=1) for c in range(bsz)], axis=0)
    ms = head_sum(o_all * o_all) * (1.0 / HEAD_DIM)
    z = dl_ref[:, :, qkv_w:].reshape(tc, DELTA_WIDTH)
    o_ref[...] = (o_all * lax.rsqrt(ms + EPS) * ng_ref[...] * _silu(z)).reshape(bsz, CHUNK, DELTA_WIDTH)


def _delta_call(dl, ba, alog_row, dtb_row, ng_row):
    bsz, s, wd = dl.shape
    n_groups = DELTA_WIDTH // GROUP_W
    chunk_spec = lambda w: pl.BlockSpec((bsz, CHUNK, w), lambda i: (0, i, 0))
    return pl.pallas_call(
        _delta_kernel,
        grid=(s // CHUNK,),
        in_specs=[
            chunk_spec(wd),
            chunk_spec(LANES),
            _const_spec((1, LANES)),
            _const_spec((1, LANES)),
            _const_spec((1, DELTA_WIDTH)),
        ],
        out_specs=chunk_spec(DELTA_WIDTH),
        out_shape=jax.ShapeDtypeStruct((bsz, s, DELTA_WIDTH), F32),
        scratch_shapes=[pltpu.VMEM((bsz * n_groups, GROUP_W, GROUP_W), F32)],
        compiler_params=_compiler_params(("arbitrary",)),
        name="delta",
    )(dl, ba, alog_row, dtb_row, ng_row)


def _out_ffn_kernel(x_ref, o1_ref, o2_ref, o3_ref, l1_ref, l2_ref, l3_ref, yd_ref, mod_ref,
                    wtop_ref, wbot_ref, g2_ref, wg_ref, wu_ref, wdn_ref, gf_ref, out_ref, natural_ref, *, final_norm):
    tm = x_ref.shape[1]
    groups = [slice(r, r + tm // FFN_ROW_GROUPS) for r in range(0, tm, tm // FFN_ROW_GROUPS)]

    staged = 0

    def natural_rows(ref):
        nonlocal staged
        n_classes = ref.shape[1]
        if n_classes == 1:
            return lambda rows: ref[0, 0, rows, :]
        slot = staged
        staged += 1
        n_slabs = ref.shape[3] // LANES
        for sl in range(n_slabs):
            for cls in range(n_classes):
                natural_ref[slot, sl, pl.ds(cls, tm // n_classes, stride=n_classes), :] = (
                    ref[0, cls, :, sl * LANES:(sl + 1) * LANES])
        return lambda rows: jnp.concatenate([natural_ref[slot, sl, rows, :] for sl in range(n_slabs)], axis=1)

    o_rows = [natural_rows(r) for r in (o1_ref, o2_ref, o3_ref)]
    l_rows = [natural_rows(r) for r in (l1_ref, l2_ref, l3_ref)]

    def merge(rows):
        lses = [get(rows) for get in l_rows]
        outs = [get(rows) for get in o_rows]
        mx = jnp.maximum(jnp.maximum(lses[0], lses[1]), lses[2])
        ws = [jnp.exp(l - mx) for l in lses]
        den = ws[0] + ws[1] + ws[2]
        return ((ws[0] * outs[0] + ws[1] * outs[1] + ws[2] * outs[2]) / den).astype(BF16)

    ya = [merge(rows) for rows in groups]
    y = [_dot(a, wtop_ref[...]) + _dot(yd_ref[0, rows, :].astype(BF16), wbot_ref[...]) for a, rows in zip(ya, groups)]
    x1 = [x_ref[0, rows, :] + mod_ref[0, 2:3, :] * yy for yy, rows in zip(y, groups)]
    hb = [(_rmsnorm_rows(xx, g2_ref[...]) * (1.0 + mod_ref[0, 4:5, :]) + mod_ref[0, 3:4, :]).astype(BF16) for xx in x1]
    gate = [_dot(h, wg_ref[...]) for h in hb]
    up = [_dot(h, wu_ref[...]) for h in hb]
    y2 = [_dot((_silu(g) * u).astype(BF16), wdn_ref[...]) for g, u in zip(gate, up)]
    for rows, xx, yy in zip(groups, x1, y2):
        x2 = xx + mod_ref[0, 5:6, :] * yy
        out_ref[0, rows, :] = _rmsnorm_rows(x2, gf_ref[...]) if final_norm else x2


def _out_ffn_call(x, o_list, lse_list, yd, mod3, wtop, wbot, g2, wg, wu, wdn, gf, final_norm):
    bsz, s, d = x.shape
    tm = FFN_TOKEN_TILE
    aw = o_list[0].shape[-1]
    dff = wg.shape[1]
    row_spec = lambda w: pl.BlockSpec((1, tm, w), lambda b, i: (b, i, 0))
    branch_spec = lambda t: pl.BlockSpec((1, t.shape[1], tm // t.shape[1], aw), lambda b, i: (b, 0, i, 0))
    n_staged = sum(t.shape[1] > 1 for t in o_list + lse_list)
    return pl.pallas_call(
        functools.partial(_out_ffn_kernel, final_norm=final_norm),
        grid=(bsz, s // tm),
        in_specs=[row_spec(d)] + [branch_spec(t) for t in o_list + lse_list] + [row_spec(yd.shape[-1])] + [
            pl.BlockSpec((1, 6, d), lambda b, i: (b, 0, 0)),
            _const_spec(wtop.shape), _const_spec(wbot.shape), _const_spec((1, d)),
            _const_spec((d, dff)), _const_spec((d, dff)), _const_spec((dff, d)), _const_spec((1, d)),
        ],
        out_specs=row_spec(d),
        out_shape=jax.ShapeDtypeStruct((bsz, s, d), x.dtype),
        scratch_shapes=[pltpu.VMEM((max(n_staged, 1), aw // LANES, tm, LANES), F32)],
        compiler_params=_compiler_params(("parallel", "parallel")),
        name="out_ffn",
    )(x, *o_list, *lse_list, yd, mod3, wtop, wbot, g2, wg, wu, wdn, gf)


def kernel(x, c, w_ada, b_ada, norm_attn_g, w_in, rel_bias, conv_w, a_log, dt_bias, delta_norm_g, w_out,
           norm_ffn_g, w_gate, w_up, w_down, final_norm_g):
    bsz, s, d = x.shape
    depth = w_ada.shape[0]
    assert s % PAD_UNIT == 0 and s % CHUNK == 0 and s % TOKEN_TILE == 0
    assert all(window // dilation == BAND for window, dilation in DILATED_BRANCHES)

    bias = _bias_call(_bucket_index_tables(), rel_bias)
    lane_pad = LANES - 2 * N_DELTA_HEADS
    for l in range(depth):
        mod3 = _mod_call(c, w_ada[l], b_ada[l]).reshape(bsz, 6, d)
        qkv_w = 3 * ATTN_WIDTH
        dl_w = 4 * DELTA_WIDTH
        conv_cols = 3 * DELTA_WIDTH
        wa = w_in[l][:, :qkv_w].astype(BF16)
        wc = w_in[l][:, qkv_w:qkv_w + conv_cols].astype(BF16)
        wz = w_in[l][:, qkv_w + conv_cols:qkv_w + dl_w].astype(BF16)
        wba = jnp.pad(w_in[l][:, qkv_w + dl_w:], ((0, 0), (0, lane_pad))).astype(BF16)
        *qkv_by_branch, dl, ba = _inproj_call(
            x, mod3, norm_attn_g[l].reshape(1, d), wa, wc, wz, wba, conv_w[l].astype(F32))

        o_list, lse_list = [], []
        for branch, ((_, dilation), qkv) in enumerate(zip(DILATED_BRANCHES, qkv_by_branch)):
            out_classes = max(dilation // MAX_FAST_ROW_STRIDE, 1)
            o, lse = _attn_call(qkv.reshape(bsz, qkv.shape[1], s, LANES), bias, branch, dilation, out_classes)
            o_list.append(o)
            lse_list.append(lse)

        gate_pad = lambda v: jnp.pad(v.astype(F32), (N_DELTA_HEADS, LANES - 2 * N_DELTA_HEADS)).reshape(1, LANES)
        yd = _delta_call(dl, ba, gate_pad(a_log[l]), gate_pad(dt_bias[l]),
                         jnp.tile(delta_norm_g[l].astype(F32), N_DELTA_HEADS).reshape(1, DELTA_WIDTH))

        wo = w_out[l].astype(BF16)
        x = _out_ffn_call(
            x, o_list, lse_list, yd, mod3, wo[:ATTN_WIDTH], wo[ATTN_WIDTH:], norm_ffn_g[l].reshape(1, d),
            w_gate[l].astype(BF16), w_up[l].astype(BF16), w_down[l].astype(BF16), final_norm_g.reshape(1, d),
            final_norm=(l == depth - 1))
    return x
```

```python
import functools
import math

import jax
import jax.numpy as jnp
import numpy as np
from jax import lax
from jax.experimental import pallas as pl
from jax.experimental.pallas import tpu as pltpu

F32 = jnp.float32
BF16 = jnp.bfloat16

HEAD_DIM = 64
N_ATTN_HEADS = 8
N_DELTA_HEADS = 8
ATTN_WIDTH = N_ATTN_HEADS * HEAD_DIM
DELTA_WIDTH = N_DELTA_HEADS * HEAD_DIM
DILATED_BRANCHES = ((128, 1), (512, 4), (2048, 16))
BAND = 128
D4, D16 = DILATED_BRANCHES[1][1], DILATED_BRANCHES[2][1]
PAD_UNIT = 2048
N_BUCKETS = 32
MAX_DISTANCE = 2048
CONV_WIDTH = 4
CHUNK = 64
EPS = 1e-6
NEG_INF = -1e30

LANES = 128
MXU_COLS = 256
MAX_FAST_ROW_STRIDE = 4
VMEM_LIMIT_BYTES = 60000 * 1024

TOKEN_TILE = 512
HALO = 16
FFN_TOKEN_TILE = 512
FFN_ROW_GROUPS = 2
ATTN_UNROLL = 8
HEADS_PER_GROUP = 4
GROUP_W = HEADS_PER_GROUP * HEAD_DIM


def _dot(a, b):
    return jnp.dot(a, b, preferred_element_type=F32)


def _dot_nt(a, b):
    return lax.dot_general(a, b, (((1,), (1,)), ((), ())), preferred_element_type=F32)


def _split3(x):
    x1 = x.astype(BF16)
    r1 = x - x1.astype(F32)
    x2 = r1.astype(BF16)
    x3 = (r1 - x2.astype(F32)).astype(BF16)
    return x1, x2, x3


def _split2(x):
    x1 = x.astype(BF16)
    x2 = (x - x1.astype(F32)).astype(BF16)
    return x1, x2


def _sigmoid(x):
    return 1.0 / (1.0 + jnp.exp(-x))


def _silu(x):
    return x * _sigmoid(x)


def _head_of(idx):
    return lax.shift_right_logical(idx, int(math.log2(HEAD_DIM)))


def _within_head(idx):
    return lax.bitwise_and(idx, HEAD_DIM - 1)


def _compiler_params(semantics):
    return pltpu.CompilerParams(dimension_semantics=semantics, vmem_limit_bytes=VMEM_LIMIT_BYTES)


def _const_spec(shape):
    zeros = (0,) * len(shape)
    return pl.BlockSpec(shape, lambda *_: zeros, pipeline_mode=pl.Buffered(1))


def _mod_kernel(c_ref, w_ref, b_ref, o_ref):
    ca = _silu(c_ref[...])
    acc = b_ref[...]
    w = w_ref[...]
    for part in _split2(ca):
        for wpart in _split2(w):
            acc = acc + _dot(part, wpart)
    o_ref[...] = acc


def _mod_call(c, w_ada, b_ada):
    bsz, d = c.shape
    n = w_ada.shape[1]
    blk = d
    return pl.pallas_call(
        _mod_kernel,
        grid=(n // blk,),
        in_specs=[
            pl.BlockSpec((bsz, d), lambda j: (0, 0)),
            pl.BlockSpec((d, blk), lambda j: (0, j)),
            pl.BlockSpec((1, blk), lambda j: (0, j)),
        ],
        out_specs=pl.BlockSpec((bsz, blk), lambda j: (0, j)),
        out_shape=jax.ShapeDtypeStruct((bsz, n), F32),
        compiler_params=_compiler_params(("arbitrary",)),
        name="mod",
    )(c, w_ada, b_ada.reshape(1, n))


def _rmsnorm_rows(x, g):
    return x * lax.rsqrt(jnp.mean(x * x, axis=-1, keepdims=True) + EPS) * g


def _inproj_kernel(x_ref, xh_ref, mod_ref, g_ref, wa_ref, wc_ref, wz_ref, wba_ref, cw_ref,
                   qkv1_ref, qkv4_ref, qkv16_ref, dl_ref, ba_ref, hb_ref, *block_scratch):
    tm = x_ref.shape[1]
    conv_w = wc_ref.shape[1]
    n_attn_blocks = wa_ref.shape[1] // MXU_COLS
    stage_refs, pre_refs = block_scratch[:2 * n_attn_blocks], block_scratch[2 * n_attn_blocks:]
    xe = jnp.concatenate([xh_ref[0], x_ref[0]], axis=0)
    h = _rmsnorm_rows(xe, g_ref[...]) * (1.0 + mod_ref[0, 1:2, :]) + mod_ref[0, 0:1, :]
    hb_ref[...] = h.astype(BF16)

    n_slabs = wa_ref.shape[1] // LANES

    def attn_block(j0):
        blk = _dot(hb_ref[HALO:, :], wa_ref[:, j0 * LANES:j0 * LANES + MXU_COLS])
        stage_ref, stage4_ref = stage_refs[2 * (j0 * LANES // MXU_COLS):][:2]
        for j in range(j0, j0 + MXU_COLS // LANES):
            jj = j - j0
            slab = blk[:, jj * LANES:(jj + 1) * LANES]
            stage_ref[jj] = slab
            qkv1_ref[0, j] = slab.astype(BF16)
            for r in range(D4):
                cls = stage_ref[jj, pl.ds(r, tm // D4, stride=D4), :]
                stage4_ref[jj, r] = cls
                qkv4_ref[0, j, r] = cls.astype(BF16)
            for r in range(D16):
                sub = stage4_ref[jj, r % D4, pl.ds(r // D4, tm // D16, stride=D16 // D4), :]
                qkv16_ref[0, j, r] = sub.astype(BF16)

    row = lax.broadcasted_iota(jnp.int32, (tm + HALO, MXU_COLS), 0)
    no_history = (row < HALO) & (pl.program_id(1) == 0)

    def conv_block(c0):
        cols = slice(c0, c0 + MXU_COLS)
        pre = jnp.where(no_history, 0.0, _dot(hb_ref[...], wc_ref[:, cols]))
        pre_ref, out_stage_ref = pre_refs[2 * (c0 // MXU_COLS):][:2]
        for sl in range(MXU_COLS // LANES):
            lanes = slice(sl * LANES, (sl + 1) * LANES)
            pre_ref[sl] = pre[:, lanes]
            taps = [cw_ref[k:k + 1, c0 + sl * LANES:c0 + (sl + 1) * LANES] for k in range(CONV_WIDTH)]
            for parity in range(2):
                conv = None
                for k in range(CONV_WIDTH):
                    start = HALO + parity - (CONV_WIDTH - 1) + k
                    term = pre_ref[sl, pl.ds(start, tm // 2, stride=2), :] * taps[k]
                    conv = term if conv is None else conv + term
                out_stage_ref[sl, pl.ds(parity, tm // 2, stride=2), :] = _silu(conv)
            dl_ref[0, :, c0 + sl * LANES:c0 + (sl + 1) * LANES] = out_stage_ref[sl]

    attn_starts = list(range(0, n_slabs, MXU_COLS // LANES))
    conv_starts = list(range(0, conv_w, MXU_COLS))
    while attn_starts or conv_starts:
        if conv_starts:
            conv_block(conv_starts.pop(0))
        if attn_starts:
            attn_block(attn_starts.pop(0))
    for c0 in range(0, wz_ref.shape[1], MXU_COLS):
        dl_ref[0, :, conv_w + c0:conv_w + c0 + MXU_COLS] = _dot(hb_ref[HALO:, :], wz_ref[:, c0:c0 + MXU_COLS])
    ba_ref[0] = _dot(hb_ref[HALO:, :], wba_ref[...])


def _inproj_call(x, mod3, g, wa, wc, wz, wba, conv_w):
    bsz, s, d = x.shape
    tm = TOKEN_TILE
    na, nc, nz, nba = wa.shape[1], wc.shape[1], wz.shape[1], wba.shape[1]
    n_slabs = na // LANES
    d4, d16 = DILATED_BRANCHES[1][1], DILATED_BRANCHES[2][1]
    row_spec = lambda w: pl.BlockSpec((1, tm, w), lambda b, i: (b, i, 0))
    cm_spec = lambda dil: pl.BlockSpec((1, n_slabs, dil, tm // dil, LANES), lambda b, i: (b, 0, 0, i, 0))
    cm_shape = lambda dil: jax.ShapeDtypeStruct((bsz, n_slabs, dil, s // dil, LANES), BF16)
    return pl.pallas_call(
        _inproj_kernel,
        grid=(bsz, s // tm),
        in_specs=[
            row_spec(d),
            pl.BlockSpec((1, HALO, d), lambda b, i: (b, jnp.maximum(i * (tm // HALO) - 1, 0), 0)),
            pl.BlockSpec((1, 6, d), lambda b, i: (b, 0, 0)),
            _const_spec((1, d)),
            _const_spec((d, na)),
            _const_spec((d, nc)),
            _const_spec((d, nz)),
            _const_spec((d, nba)),
            _const_spec(conv_w.shape),
        ],
        out_specs=[
            pl.BlockSpec((1, n_slabs, tm, LANES), lambda b, i: (b, 0, i, 0)),
            cm_spec(d4),
            cm_spec(d16),
            row_spec(nc + nz),
            row_spec(nba),
        ],
        out_shape=[
            jax.ShapeDtypeStruct((bsz, n_slabs, s, LANES), BF16),
            cm_shape(d4),
            cm_shape(d16),
            jax.ShapeDtypeStruct((bsz, s, nc + nz), F32),
            jax.ShapeDtypeStruct((bsz, s, nba), F32),
        ],
        scratch_shapes=(
            [pltpu.VMEM((tm + HALO, d), BF16)]
            + [pltpu.VMEM((MXU_COLS // LANES, tm, LANES), F32),
               pltpu.VMEM((MXU_COLS // LANES, D4, tm // D4, LANES), F32)] * (na // MXU_COLS)
            + [pltpu.VMEM((MXU_COLS // LANES, tm + HALO, LANES), F32),
               pltpu.VMEM((MXU_COLS // LANES, tm, LANES), F32)] * (nc // MXU_COLS)
        ),
        compiler_params=_compiler_params(("parallel", "parallel")),
        name="inproj",
    )(x, x, mod3, g, wa, wc, wz, wba, conv_w)


def _t5_bucket(distance):
    max_exact = N_BUCKETS // 2
    dist_f = jnp.maximum(distance, 1).astype(F32)
    large = max_exact + (jnp.log(dist_f / max_exact) / math.log(MAX_DISTANCE / max_exact)
                         * (N_BUCKETS - max_exact)).astype(jnp.int32)
    return jnp.where(distance < max_exact, distance, jnp.minimum(large, N_BUCKETS - 1))


def _bucket_index_tables():
    qi = jnp.arange(BAND)[:, None]
    kj = jnp.arange(2 * BAND)[None, :]
    steps = qi + BAND - kj
    in_window = (steps >= 0) & (steps <= BAND)
    tables = [jnp.where(in_window, _t5_bucket(jnp.maximum(steps, 0) * dilation), -1)
              for _, dilation in DILATED_BRANCHES]
    return jnp.stack(tables).astype(jnp.int32)


def _bias_kernel(idx_ref, rb_ref, o_ref):
    idx = idx_ref[0]
    in_current = lax.broadcasted_iota(jnp.int32, idx.shape, 1) >= BAND
    masked_half = jnp.full((BAND, BAND), NEG_INF, F32)
    for h in range(N_ATTN_HEADS):
        acc = jnp.full(idx.shape, NEG_INF, F32)
        for b in range(N_BUCKETS):
            acc = jnp.where(idx == b, rb_ref[b, h], acc)
        o_ref[0, h, 0] = acc
        o_ref[0, h, 1] = jnp.where(in_current, acc, NEG_INF)
        o_ref[0, h, 2] = jnp.concatenate([acc[:, BAND:], masked_half], axis=1)


def _bias_call(idx_tables, rel_bias):
    nbr = idx_tables.shape[0]
    blk = (3,) + idx_tables.shape[1:]
    return pl.pallas_call(
        _bias_kernel,
        grid=(nbr,),
        in_specs=[
            pl.BlockSpec((1,) + idx_tables.shape[1:], lambda i: (i, 0, 0)),
            pl.BlockSpec(memory_space=pltpu.SMEM),
        ],
        out_specs=pl.BlockSpec((1, N_ATTN_HEADS) + blk, lambda i: (i, 0, 0, 0, 0)),
        out_shape=jax.ShapeDtypeStruct((nbr, N_ATTN_HEADS) + blk, F32),
        compiler_params=_compiler_params(("arbitrary",)),
        name="bias",
    )(idx_tables, rel_bias.astype(F32))


def _attn_kernel(q_ref, k_ref, v_ref, bias_ref, o_ref, lse_ref, *, dilation, out_classes, blocks_per_class,
                 n_blocks):
    lane = lax.broadcasted_iota(jnp.int32, (BAND, LANES), 1)
    is_a = lane < HEAD_DIM
    class_shift = int(math.log2(blocks_per_class))
    out_stride = dilation // out_classes

    def body(jg, carry):
        blocks = []
        for u in range(ATTN_UNROLL):
            j = jg * ATTN_UNROLL + u
            r0 = pl.multiple_of(j * BAND, BAND)
            ks = pl.multiple_of(jnp.maximum(j - 1, 0) * BAND, BAND)
            n_in_class = lax.bitwise_and(j, blocks_per_class - 1)
            variant = jnp.where(n_in_class == 0, jnp.where(j > 0, 1, 2), 0)
            q = q_ref[0, 0, pl.ds(r0, BAND), :] * (HEAD_DIM ** -0.5)
            res_class = lax.shift_right_logical(j, class_shift)
            blocks.append(dict(
                q=q, kw=k_ref[0, 0, pl.ds(ks, 2 * BAND), :], vw=v_ref[0, 0, pl.ds(ks, 2 * BAND), :], variant=variant,
                out_class=lax.bitwise_and(res_class, out_classes - 1),
                row0=lax.shift_right_logical(res_class, int(math.log2(out_classes))) + out_stride * BAND * n_in_class))
        chains = [(blk, hh) for blk in blocks for hh in range(2)]
        zero = jnp.zeros((BAND, LANES), BF16)
        s_all = [_dot_nt(jnp.where(is_a if hh == 0 else jnp.logical_not(is_a), blk["q"], zero), blk["kw"])
                 + bias_ref[0, hh, blk["variant"]] for blk, hh in chains]
        m_all = [jnp.max(s, axis=-1, keepdims=True) for s in s_all]
        e_all = [jnp.exp(s - m) for s, m in zip(s_all, m_all)]
        d_all = [jnp.sum(e, axis=-1, keepdims=True) for e in e_all]
        o_all = [_dot(e.astype(BF16), blk["vw"]) / d for e, d, (blk, _) in zip(e_all, d_all, chains)]
        lse_all = [jnp.broadcast_to(m + jnp.log(d), (BAND, LANES)) for m, d in zip(m_all, d_all)]
        for u, blk in enumerate(blocks):
            rows = pl.ds(blk["row0"], BAND, stride=out_stride)
            o_ref[0, blk["out_class"], rows, :] = jnp.where(is_a, o_all[2 * u], o_all[2 * u + 1])
            lse_ref[0, blk["out_class"], rows, :] = jnp.where(is_a, lse_all[2 * u], lse_all[2 * u + 1])
        return carry

    lax.fori_loop(0, n_blocks // ATTN_UNROLL, body, 0)


def _attn_call(qkv, bias, branch, dilation, out_classes):
    bsz, n_slabs, s, _ = qkv.shape
    hp = n_slabs // 3
    n_blocks = s // BAND
    blocks_per_class = n_blocks // dilation
    assert blocks_per_class & (blocks_per_class - 1) == 0 and n_blocks % ATTN_UNROLL == 0
    assert dilation % out_classes == 0 and out_classes & (out_classes - 1) == 0
    kern = functools.partial(_attn_kernel, dilation=dilation, out_classes=out_classes,
                             blocks_per_class=blocks_per_class, n_blocks=n_blocks)
    slab_spec = lambda t: pl.BlockSpec((1, 1, s, LANES), lambda h, b: (b, t * hp + h, 0, 0))
    out_spec = pl.BlockSpec((1, out_classes, s // out_classes, LANES), lambda h, b: (b, 0, 0, h))
    return pl.pallas_call(
        kern,
        grid=(hp, bsz),
        in_specs=[
            slab_spec(0), slab_spec(1), slab_spec(2),
            pl.BlockSpec((1, 2, 3, BAND, 2 * BAND), lambda h, b: (branch, h, 0, 0, 0)),
        ],
        out_specs=[out_spec, out_spec],
        out_shape=[jax.ShapeDtypeStruct((bsz, out_classes, s // out_classes, hp * LANES), F32)] * 2,
        compiler_params=_compiler_params(("parallel", "parallel")),
        name=f"attn_d{dilation}",
    )(qkv, qkv, qkv, bias)


def _bd_stack(x):
    xb = x.astype(BF16)
    head_of_lane = _head_of(lax.broadcasted_iota(jnp.int32, xb.shape, 1))
    zero = jnp.zeros_like(xb)
    return jnp.concatenate([jnp.where(head_of_lane == h, xb, zero) for h in range(HEADS_PER_GROUP)], axis=0)


def _delta_kernel(dl_ref, ba_ref, alog_ref, dtb_ref, ng_ref, o_ref, state_ref):
    bsz = dl_ref.shape[0]
    tc = bsz * CHUNK
    qkv_w = 3 * DELTA_WIDTH
    n_groups = DELTA_WIDTH // GROUP_W

    @pl.when(pl.program_id(0) == 0)
    def _():
        state_ref[...] = jnp.zeros_like(state_ref)

    q_raw = dl_ref[:, :, :DELTA_WIDTH].reshape(tc, DELTA_WIDTH)
    k_raw = dl_ref[:, :, DELTA_WIDTH:2 * DELTA_WIDTH].reshape(tc, DELTA_WIDTH)
    v_all = dl_ref[:, :, 2 * DELTA_WIDTH:qkv_w].reshape(tc, DELTA_WIDTH)

    r512 = _head_of(lax.broadcasted_iota(jnp.int32, (DELTA_WIDTH, DELTA_WIDTH), 0))
    c512 = _head_of(lax.broadcasted_iota(jnp.int32, (DELTA_WIDTH, DELTA_WIDTH), 1))
    head_ones = (r512 == c512).astype(BF16)

    def head_sum(x):
        return _dot(x.astype(BF16), head_ones)

    q_all = q_raw * lax.rsqrt(head_sum(q_raw * q_raw) + EPS) * (HEAD_DIM ** -0.5)
    k_all = k_raw * lax.rsqrt(head_sum(k_raw * k_raw) + EPS)

    t = ba_ref[...].reshape(tc, LANES)
    za = t + dtb_ref[...]
    softplus = jnp.maximum(za, 0.0) + jnp.log1p(jnp.exp(-jnp.abs(za)))
    g_lane = -jnp.exp(alog_ref[...]) * softplus
    er = lax.broadcasted_iota(jnp.int32, (LANES, DELTA_WIDTH), 0)
    ec = _head_of(lax.broadcasted_iota(jnp.int32, (LANES, DELTA_WIDTH), 1))
    expand_beta = (er == ec).astype(BF16)
    expand_g = (er == ec + N_DELTA_HEADS).astype(BF16)
    beta_all = _dot(_sigmoid(t).astype(BF16), expand_beta)
    g_all = jnp.zeros((tc, DELTA_WIDTH), F32)
    for part in _split3(g_lane):
        g_all = g_all + _dot(part, expand_g)

    ci = lax.broadcasted_iota(jnp.int32, (CHUNK, CHUNK), 0)
    cj = lax.broadcasted_iota(jnp.int32, (CHUNK, CHUNK), 1)
    lower_incl = (ci >= cj).astype(BF16)
    all_ones = jnp.ones((CHUNK, CHUNK), BF16)
    row_w = lax.broadcasted_iota(jnp.int32, (CHUNK, DELTA_WIDTH), 0)
    col_w = _within_head(lax.broadcasted_iota(jnp.int32, (CHUNK, DELTA_WIDTH), 1))
    upper_incl_w = row_w <= col_w
    row_g = lax.broadcasted_iota(jnp.int32, (CHUNK, GROUP_W), 0)
    col_g = _within_head(lax.broadcasted_iota(jnp.int32, (CHUNK, GROUP_W), 1))
    causal_g = row_g >= col_g
    strict_g = row_g > col_g
    eye_g = (row_g == col_g).astype(F32)
    level_masks = []
    for n in range(int(math.log2(CHUNK))):
        shr = lambda v, k: lax.shift_right_logical(v, k)
        level_masks.append((shr(row_g, n + 1) == shr(col_g, n + 1)) & (shr(row_g, n) > shr(col_g, n)))
    sr = _head_of(lax.broadcasted_iota(jnp.int32, (GROUP_W, GROUP_W), 0))
    sc = _head_of(lax.broadcasted_iota(jnp.int32, (GROUP_W, GROUP_W), 1))
    state_mask = sr == sc

    chains = []
    for c in range(bsz):
        rows = slice(c * CHUNK, (c + 1) * CHUNK)
        gb = g_all[rows]
        gcum = jnp.zeros((CHUNK, DELTA_WIDTH), F32)
        gcum_t = jnp.zeros((CHUNK, DELTA_WIDTH), F32)
        for part in _split3(gb):
            gcum = gcum + _dot(lower_incl, part)
        for part in _split3(jnp.where(upper_incl_w, gb, 0.0)):
            gcum_t = gcum_t + _dot(all_ones, part)
        g_last = gcum[CHUNK - 1:CHUNK, :]
        exp_g = jnp.exp(gcum)
        exp_dec = jnp.exp(g_last - gcum)
        exp_last = jnp.exp(g_last)

        kc, qc, vc, bb = k_all[rows], q_all[rows], v_all[rows], beta_all[rows]
        kb = kc * bb
        vb = vc * bb
        kbe = kb * exp_g
        qe = qc * exp_g
        kdec = kc * exp_dec
        for g in range(n_groups):
            gl = slice(g * GROUP_W, (g + 1) * GROUP_W)
            diff = gcum[:, gl] - gcum_t[:, gl]
            decay = jnp.where(causal_g, jnp.exp(jnp.where(causal_g, diff, 0.0)), 0.0)
            aq = _dot_nt(jnp.concatenate([kb[:, gl], qc[:, gl]], axis=0).astype(BF16), _bd_stack(kc[:, gl]))
            chains.append(dict(
                slot=c * n_groups + g, a=jnp.where(strict_g, aq[:CHUNK] * decay, 0.0),
                qk=(aq[CHUNK:] * decay).astype(BF16), qe=qe[:, gl], kdec_t=kdec[:, gl].T.astype(BF16),
                exp_last=exp_last[:, gl],
                rhs=jnp.concatenate([_bd_stack(vb[:, gl]), _bd_stack(kbe[:, gl])], axis=1)))

    t_invs = [eye_g - jnp.where(level_masks[0], ch["a"], 0.0) for ch in chains]
    for mask in level_masks[1:]:
        ys = [_dot(jnp.where(mask, ch["a"], 0.0).astype(BF16), _bd_stack(t)) for ch, t in zip(chains, t_invs)]
        t_invs = [t - _dot(t.astype(BF16), _bd_stack(y)) for t, y in zip(t_invs, ys)]
    uws = [_dot(t.astype(BF16), ch["rhs"]) for ch, t in zip(chains, t_invs)]

    states = [state_ref[ch["slot"]] for ch in chains]
    wss = [_dot(jnp.concatenate([uw[:, GROUP_W:], ch["qe"]], axis=0).astype(BF16), st.astype(BF16))
           for ch, uw, st in zip(chains, uws, states)]
    v_news = [uw[:, :GROUP_W] - ws[:CHUNK] for uw, ws in zip(uws, wss)]
    outs = [ws[CHUNK:] + _dot(ch["qk"], _bd_stack(vn)) for ch, ws, vn in zip(chains, wss, v_news)]
    upds = [_dot(ch["kdec_t"], vn.astype(BF16)) for ch, vn in zip(chains, v_news)]
    for ch, st, upd in zip(chains, states, upds):
        for quad in (slice(0, GROUP_W // 2), slice(GROUP_W // 2, GROUP_W)):
            state_ref[ch["slot"], quad, quad] = (
                st[quad, quad] * ch["exp_last"][:, quad] + jnp.where(state_mask[quad, quad], upd[quad, quad], 0.0))

    o_all = jnp.concatenate(
        [jnp.concatenate(outs[c * n_groups:(c + 1) * n_groups], axis=1) for c in range(bsz)], axis=0)
    ms = head_sum(o_all * o_all) * (1.0 / HEAD_DIM)
    z = dl_ref[:, :, qkv_w:].reshape(tc, DELTA_WIDTH)
    o_ref[...] = (o_all * lax.rsqrt(ms + EPS) * ng_ref[...] * _silu(z)).reshape(bsz, CHUNK, DELTA_WIDTH)


def _delta_call(dl, ba, alog_row, dtb_row, ng_row):
    bsz, s, wd = dl.shape
    n_groups = DELTA_WIDTH // GROUP_W
    chunk_spec = lambda w: pl.BlockSpec((bsz, CHUNK, w), lambda i: (0, i, 0))
    return pl.pallas_call(
        _delta_kernel,
        grid=(s // CHUNK,),
        in_specs=[
            chunk_spec(wd),
            chunk_spec(LANES),
            _const_spec((1, LANES)),
            _const_spec((1, LANES)),
            _const_spec((1, DELTA_WIDTH)),
        ],
        out_specs=chunk_spec(DELTA_WIDTH),
        out_shape=jax.ShapeDtypeStruct((bsz, s, DELTA_WIDTH), F32),
        scratch_shapes=[pltpu.VMEM((bsz * n_groups, GROUP_W, GROUP_W), F32)],
        compiler_params=_compiler_params(("arbitrary",)),
        name="delta",
    )(dl, ba, alog_row, dtb_row, ng_row)


def _out_ffn_kernel(x_ref, o1_ref, o2_ref, o3_ref, l1_ref, l2_ref, l3_ref, yd_ref, mod_ref,
                    wtop_ref, wbot_ref, g2_ref, wg_ref, wu_ref, wdn_ref, gf_ref, out_ref, natural_ref, *, final_norm):
    tm = x_ref.shape[1]
    groups = [slice(r, r + tm // FFN_ROW_GROUPS) for r in range(0, tm, tm // FFN_ROW_GROUPS)]

    staged = 0

    def natural_rows(ref):
        nonlocal staged
        n_classes = ref.shape[1]
        if n_classes == 1:
            return lambda rows: ref[0, 0, rows, :]
        slot = staged
        staged += 1
        n_slabs = ref.shape[3] // LANES
        for sl in range(n_slabs):
            for cls in range(n_classes):
                natural_ref[slot, sl, pl.ds(cls, tm // n_classes, stride=n_classes), :] = (
                    ref[0, cls, :, sl * LANES:(sl + 1) * LANES])
        return lambda rows: jnp.concatenate([natural_ref[slot, sl, rows, :] for sl in range(n_slabs)], axis=1)

    o_rows = [natural_rows(r) for r in (o1_ref, o2_ref, o3_ref)]
    l_rows = [natural_rows(r) for r in (l1_ref, l2_ref, l3_ref)]

    def merge(rows):
        lses = [get(rows) for get in l_rows]
        outs = [get(rows) for get in o_rows]
        mx = jnp.maximum(jnp.maximum(lses[0], lses[1]), lses[2])
        ws = [jnp.exp(l - mx) for l in lses]
        den = ws[0] + ws[1] + ws[2]
        return ((ws[0] * outs[0] + ws[1] * outs[1] + ws[2] * outs[2]) / den).astype(BF16)

    ya = [merge(rows) for rows in groups]
    y = [_dot(a, wtop_ref[...]) + _dot(yd_ref[0, rows, :].astype(BF16), wbot_ref[...]) for a, rows in zip(ya, groups)]
    x1 = [x_ref[0, rows, :] + mod_ref[0, 2:3, :] * yy for yy, rows in zip(y, groups)]
    hb = [(_rmsnorm_rows(xx, g2_ref[...]) * (1.0 + mod_ref[0, 4:5, :]) + mod_ref[0, 3:4, :]).astype(BF16) for xx in x1]
    gate = [_dot(h, wg_ref[...]) for h in hb]
    up = [_dot(h, wu_ref[...]) for h in hb]
    y2 = [_dot((_silu(g) * u).astype(BF16), wdn_ref[...]) for g, u in zip(gate, up)]
    for rows, xx, yy in zip(groups, x1, y2):
        x2 = xx + mod_ref[0, 5:6, :] * yy
        out_ref[0, rows, :] = _rmsnorm_rows(x2, gf_ref[...]) if final_norm else x2


def _out_ffn_call(x, o_list, lse_list, yd, mod3, wtop, wbot, g2, wg, wu, wdn, gf, final_norm):
    bsz, s, d = x.shape
    tm = FFN_TOKEN_TILE
    aw = o_list[0].shape[-1]
    dff = wg.shape[1]
    row_spec = lambda w: pl.BlockSpec((1, tm, w), lambda b, i: (b, i, 0))
    branch_spec = lambda t: pl.BlockSpec((1, t.shape[1], tm // t.shape[1], aw), lambda b, i: (b, 0, i, 0))
    n_staged = sum(t.shape[1] > 1 for t in o_list + lse_list)
    return pl.pallas_call(
        functools.partial(_out_ffn_kernel, final_norm=final_norm),
        grid=(bsz, s // tm),
        in_specs=[row_spec(d)] + [branch_spec(t) for t in o_list + lse_list] + [row_spec(yd.shape[-1])] + [
            pl.BlockSpec((1, 6, d), lambda b, i: (b, 0, 0)),
            _const_spec(wtop.shape), _const_spec(wbot.shape), _const_spec((1, d)),
            _const_spec((d, dff)), _const_spec((d, dff)), _const_spec((dff, d)), _const_spec((1, d)),
        ],
        out_specs=row_spec(d),
        out_shape=jax.ShapeDtypeStruct((bsz, s, d), x.dtype),
        scratch_shapes=[pltpu.VMEM((max(n_staged, 1), aw // LANES, tm, LANES), F32)],
        compiler_params=_compiler_params(("parallel", "parallel")),
        name="out_ffn",
    )(x, *o_list, *lse_list, yd, mod3, wtop, wbot, g2, wg, wu, wdn, gf)


def kernel(x, c, w_ada, b_ada, norm_attn_g, w_in, rel_bias, conv_w, a_log, dt_bias, delta_norm_g, w_out,
           norm_ffn_g, w_gate, w_up, w_down, final_norm_g):
    bsz, s, d = x.shape
    depth = w_ada.shape[0]
    assert s % PAD_UNIT == 0 and s % CHUNK == 0 and s % TOKEN_TILE == 0
    assert all(window // dilation == BAND for window, dilation in DILATED_BRANCHES)

    bias = _bias_call(_bucket_index_tables(), rel_bias)
    lane_pad = LANES - 2 * N_DELTA_HEADS
    for l in range(depth):
        mod3 = _mod_call(c, w_ada[l], b_ada[l]).reshape(bsz, 6, d)
        qkv_w = 3 * ATTN_WIDTH
        dl_w = 4 * DELTA_WIDTH
        conv_cols = 3 * DELTA_WIDTH
        wa = w_in[l][:, :qkv_w].astype(BF16)
        wc = w_in[l][:, qkv_w:qkv_w + conv_cols].astype(BF16)
        wz = w_in[l][:, qkv_w + conv_cols:qkv_w + dl_w].astype(BF16)
        wba = jnp.pad(w_in[l][:, qkv_w + dl_w:], ((0, 0), (0, lane_pad))).astype(BF16)
        *qkv_by_branch, dl, ba = _inproj_call(
            x, mod3, norm_attn_g[l].reshape(1, d), wa, wc, wz, wba, conv_w[l].astype(F32))

        o_list, lse_list = [], []
        for branch, ((_, dilation), qkv) in enumerate(zip(DILATED_BRANCHES, qkv_by_branch)):
            out_classes = max(dilation // MAX_FAST_ROW_STRIDE, 1)
            o, lse = _attn_call(qkv.reshape(bsz, qkv.shape[1], s, LANES), bias, branch, dilation, out_classes)
            o_list.append(o)
            lse_list.append(lse)

        gate_pad = lambda v: jnp.pad(v.astype(F32), (N_DELTA_HEADS, LANES - 2 * N_DELTA_HEADS)).reshape(1, LANES)
        yd = _delta_call(dl, ba, gate_pad(a_log[l]), gate_pad(dt_bias[l]),
                         jnp.tile(delta_norm_g[l].astype(F32), N_DELTA_HEADS).reshape(1, DELTA_WIDTH))

        wo = w_out[l].astype(BF16)
        x = _out_ffn_call(
            x, o_list, lse_list, yd, mod3, wo[:ATTN_WIDTH], wo[ATTN_WIDTH:], norm_ffn_g[l].reshape(1, d),
            w_gate[l].astype(BF16), w_up[l].astype(BF16), w_down[l].astype(BF16), final_norm_g.reshape(1, d),
            final_norm=(l == depth - 1))
    return x
```

```python
import functools
import math

import jax
import jax.numpy as jnp
import numpy as np
from jax import lax
from jax.experimental import pallas as pl
from jax.experimental.pallas import tpu as pltpu

F32 = jnp.float32
BF16 = jnp.bfloat16

HEAD_DIM = 64
N_ATTN_HEADS = 8
N_DELTA_HEADS = 8
ATTN_WIDTH = N_ATTN_HEADS * HEAD_DIM
DELTA_WIDTH = N_DELTA_HEADS * HEAD_DIM
DILATED_BRANCHES = ((128, 1), (512, 4), (2048, 16))
BAND = 128
D4, D16 = DILATED_BRANCHES[1][1], DILATED_BRANCHES[2][1]
PAD_UNIT = 2048
N_BUCKETS = 32
MAX_DISTANCE = 2048
CONV_WIDTH = 4
CHUNK = 64
EPS = 1e-6
NEG_INF = -1e30

LANES = 128
MXU_COLS = 256
MAX_FAST_ROW_STRIDE = 4
VMEM_LIMIT_BYTES = 60000 * 1024

TOKEN_TILE = 512
HALO = 16
FFN_TOKEN_TILE = 512
FFN_ROW_GROUPS = 2
ATTN_UNROLL = 8
HEADS_PER_GROUP = 2
GROUP_W = HEADS_PER_GROUP * HEAD_DIM


def _dot(a, b):
    return jnp.dot(a, b, preferred_element_type=F32)


def _dot_nt(a, b):
    return lax.dot_general(a, b, (((1,), (1,)), ((), ())), preferred_element_type=F32)


def _split3(x):
    x1 = x.astype(BF16)
    r1 = x - x1.astype(F32)
    x2 = r1.astype(BF16)
    x3 = (r1 - x2.astype(F32)).astype(BF16)
    return x1, x2, x3


def _split2(x):
    x1 = x.astype(BF16)
    x2 = (x - x1.astype(F32)).astype(BF16)
    return x1, x2


def _sigmoid(x):
    return 1.0 / (1.0 + jnp.exp(-x))


def _silu(x):
    return x * _sigmoid(x)


def _head_of(idx):
    return lax.shift_right_logical(idx, int(math.log2(HEAD_DIM)))


def _within_head(idx):
    return lax.bitwise_and(idx, HEAD_DIM - 1)


def _compiler_params(semantics):
    return pltpu.CompilerParams(dimension_semantics=semantics, vmem_limit_bytes=VMEM_LIMIT_BYTES)


def _const_spec(shape):
    zeros = (0,) * len(shape)
    return pl.BlockSpec(shape, lambda *_: zeros, pipeline_mode=pl.Buffered(1))


def _mod_kernel(c_ref, w_ref, b_ref, o_ref):
    ca = _silu(c_ref[...])
    acc = b_ref[...]
    w = w_ref[...]
    for part in _split2(ca):
        for wpart in _split2(w):
            acc = acc + _dot(part, wpart)
    o_ref[...] = acc


def _mod_call(c, w_ada, b_ada):
    bsz, d = c.shape
    n = w_ada.shape[1]
    blk = d
    return pl.pallas_call(
        _mod_kernel,
        grid=(n // blk,),
        in_specs=[
            pl.BlockSpec((bsz, d), lambda j: (0, 0)),
            pl.BlockSpec((d, blk), lambda j: (0, j)),
            pl.BlockSpec((1, blk), lambda j: (0, j)),
        ],
        out_specs=pl.BlockSpec((bsz, blk), lambda j: (0, j)),
        out_shape=jax.ShapeDtypeStruct((bsz, n), F32),
        compiler_params=_compiler_params(("arbitrary",)),
        name="mod",
    )(c, w_ada, b_ada.reshape(1, n))


def _rmsnorm_rows(x, g):
    return x * lax.rsqrt(jnp.mean(x * x, axis=-1, keepdims=True) + EPS) * g


def _inproj_kernel(x_ref, xh_ref, mod_ref, g_ref, wa_ref, wc_ref, wz_ref, wba_ref, cw_ref,
                   qkv1_ref, qkv4_ref, qkv16_ref, dl_ref, ba_ref, hb_ref, *block_scratch):
    tm = x_ref.shape[1]
    conv_w = wc_ref.shape[1]
    n_attn_blocks = wa_ref.shape[1] // MXU_COLS
    stage_refs, pre_refs = block_scratch[:2 * n_attn_blocks], block_scratch[2 * n_attn_blocks:]
    xe = jnp.concatenate([xh_ref[0], x_ref[0]], axis=0)
    h = _rmsnorm_rows(xe, g_ref[...]) * (1.0 + mod_ref[0, 1:2, :]) + mod_ref[0, 0:1, :]
    hb_ref[...] = h.astype(BF16)

    n_slabs = wa_ref.shape[1] // LANES

    def attn_block(j0):
        blk = _dot(hb_ref[HALO:, :], wa_ref[:, j0 * LANES:j0 * LANES + MXU_COLS])
        stage_ref, stage4_ref = stage_refs[2 * (j0 * LANES // MXU_COLS):][:2]
        for j in range(j0, j0 + MXU_COLS // LANES):
            jj = j - j0
            slab = blk[:, jj * LANES:(jj + 1) * LANES]
            stage_ref[jj] = slab
            qkv1_ref[0, j] = slab.astype(BF16)
            for r in range(D4):
                cls = stage_ref[jj, pl.ds(r, tm // D4, stride=D4), :]
                stage4_ref[jj, r] = cls
                qkv4_ref[0, j, r] = cls.astype(BF16)
            for r in range(D16):
                sub = stage4_ref[jj, r % D4, pl.ds(r // D4, tm // D16, stride=D16 // D4), :]
                qkv16_ref[0, j, r] = sub.astype(BF16)

    row = lax.broadcasted_iota(jnp.int32, (tm + HALO, MXU_COLS), 0)
    no_history = (row < HALO) & (pl.program_id(1) == 0)

    def conv_block(c0):
        cols = slice(c0, c0 + MXU_COLS)
        pre = jnp.where(no_history, 0.0, _dot(hb_ref[...], wc_ref[:, cols]))
        pre_ref, out_stage_ref = pre_refs[2 * (c0 // MXU_COLS):][:2]
        for sl in range(MXU_COLS // LANES):
            lanes = slice(sl * LANES, (sl + 1) * LANES)
            pre_ref[sl] = pre[:, lanes]
            taps = [cw_ref[k:k + 1, c0 + sl * LANES:c0 + (sl + 1) * LANES] for k in range(CONV_WIDTH)]
            for parity in range(2):
                conv = None
                for k in range(CONV_WIDTH):
                    start = HALO + parity - (CONV_WIDTH - 1) + k
                    term = pre_ref[sl, pl.ds(start, tm // 2, stride=2), :] * taps[k]
                    conv = term if conv is None else conv + term
                out_stage_ref[sl, pl.ds(parity, tm // 2, stride=2), :] = _silu(conv)
            dl_ref[0, :, c0 + sl * LANES:c0 + (sl + 1) * LANES] = out_stage_ref[sl]

    attn_starts = list(range(0, n_slabs, MXU_COLS // LANES))
    conv_starts = list(range(0, conv_w, MXU_COLS))
    while attn_starts or conv_starts:
        if conv_starts:
            conv_block(conv_starts.pop(0))
        if attn_starts:
            attn_block(attn_starts.pop(0))
    for c0 in range(0, wz_ref.shape[1], MXU_COLS):
        dl_ref[0, :, conv_w + c0:conv_w + c0 + MXU_COLS] = _dot(hb_ref[HALO:, :], wz_ref[:, c0:c0 + MXU_COLS])
    ba_ref[0] = _dot(hb_ref[HALO:, :], wba_ref[...])


def _inproj_call(x, mod3, g, wa, wc, wz, wba, conv_w):
    bsz, s, d = x.shape
    tm = TOKEN_TILE
    na, nc, nz, nba = wa.shape[1], wc.shape[1], wz.shape[1], wba.shape[1]
    n_slabs = na // LANES
    d4, d16 = DILATED_BRANCHES[1][1], DILATED_BRANCHES[2][1]
    row_spec = lambda w: pl.BlockSpec((1, tm, w), lambda b, i: (b, i, 0))
    cm_spec = lambda dil: pl.BlockSpec((1, n_slabs, dil, tm // dil, LANES), lambda b, i: (b, 0, 0, i, 0))
    cm_shape = lambda dil: jax.ShapeDtypeStruct((bsz, n_slabs, dil, s // dil, LANES), BF16)
    return pl.pallas_call(
        _inproj_kernel,
        grid=(bsz, s // tm),
        in_specs=[
            row_spec(d),
            pl.BlockSpec((1, HALO, d), lambda b, i: (b, jnp.maximum(i * (tm // HALO) - 1, 0), 0)),
            pl.BlockSpec((1, 6, d), lambda b, i: (b, 0, 0)),
            _const_spec((1, d)),
            _const_spec((d, na)),
            _const_spec((d, nc)),
            _const_spec((d, nz)),
            _const_spec((d, nba)),
            _const_spec(conv_w.shape),
        ],
        out_specs=[
            pl.BlockSpec((1, n_slabs, tm, LANES), lambda b, i: (b, 0, i, 0)),
            cm_spec(d4),
            cm_spec(d16),
            row_spec(nc + nz),
            row_spec(nba),
        ],
        out_shape=[
            jax.ShapeDtypeStruct((bsz, n_slabs, s, LANES), BF16),
            cm_shape(d4),
            cm_shape(d16),
            jax.ShapeDtypeStruct((bsz, s, nc + nz), F32),
            jax.ShapeDtypeStruct((bsz, s, nba), F32),
        ],
        scratch_shapes=(
            [pltpu.VMEM((tm + HALO, d), BF16)]
            + [pltpu.VMEM((MXU_COLS // LANES, tm, LANES), F32),
               pltpu.VMEM((MXU_COLS // LANES, D4, tm // D4, LANES), F32)] * (na // MXU_COLS)
            + [pltpu.VMEM((MXU_COLS // LANES, tm + HALO, LANES), F32),
               pltpu.VMEM((MXU_COLS // LANES, tm, LANES), F32)] * (nc // MXU_COLS)
        ),
        compiler_params=_compiler_params(("parallel", "parallel")),
        name="inproj",
    )(x, x, mod3, g, wa, wc, wz, wba, conv_w)


def _t5_bucket(distance):
    max_exact = N_BUCKETS // 2
    dist_f = jnp.maximum(distance, 1).astype(F32)
    large = max_exact + (jnp.log(dist_f / max_exact) / math.log(MAX_DISTANCE / max_exact)
                         * (N_BUCKETS - max_exact)).astype(jnp.int32)
    return jnp.where(distance < max_exact, distance, jnp.minimum(large, N_BUCKETS - 1))


def _bucket_index_tables():
    qi = jnp.arange(BAND)[:, None]
    kj = jnp.arange(2 * BAND)[None, :]
    steps = qi + BAND - kj
    in_window = (steps >= 0) & (steps <= BAND)
    tables = [jnp.where(in_window, _t5_bucket(jnp.maximum(steps, 0) * dilation), -1)
              for _, dilation in DILATED_BRANCHES]
    return jnp.stack(tables).astype(jnp.int32)


def _bias_kernel(idx_ref, rb_ref, o_ref):
    idx = idx_ref[0]
    in_current = lax.broadcasted_iota(jnp.int32, idx.shape, 1) >= BAND
    masked_half = jnp.full((BAND, BAND), NEG_INF, F32)
    for h in range(N_ATTN_HEADS):
        acc = jnp.full(idx.shape, NEG_INF, F32)
        for b in range(N_BUCKETS):
            acc = jnp.where(idx == b, rb_ref[b, h], acc)
        o_ref[0, h, 0] = acc
        o_ref[0, h, 1] = jnp.where(in_current, acc, NEG_INF)
        o_ref[0, h, 2] = jnp.concatenate([acc[:, BAND:], masked_half], axis=1)


def _bias_call(idx_tables, rel_bias):
    nbr = idx_tables.shape[0]
    blk = (3,) + idx_tables.shape[1:]
    return pl.pallas_call(
        _bias_kernel,
        grid=(nbr,),
        in_specs=[
            pl.BlockSpec((1,) + idx_tables.shape[1:], lambda i: (i, 0, 0)),
            pl.BlockSpec(memory_space=pltpu.SMEM),
        ],
        out_specs=pl.BlockSpec((1, N_ATTN_HEADS) + blk, lambda i: (i, 0, 0, 0, 0)),
        out_shape=jax.ShapeDtypeStruct((nbr, N_ATTN_HEADS) + blk, F32),
        compiler_params=_compiler_params(("arbitrary",)),
        name="bias",
    )(idx_tables, rel_bias.astype(F32))


def _attn_kernel(q_ref, k_ref, v_ref, bias_ref, o_ref, lse_ref, *, dilation, out_classes, blocks_per_class,
                 n_blocks):
    lane = lax.broadcasted_iota(jnp.int32, (BAND, LANES), 1)
    is_a = lane < HEAD_DIM
    class_shift = int(math.log2(blocks_per_class))
    out_stride = dilation // out_classes

    def body(jg, carry):
        blocks = []
        for u in range(ATTN_UNROLL):
            j = jg * ATTN_UNROLL + u
            r0 = pl.multiple_of(j * BAND, BAND)
            ks = pl.multiple_of(jnp.maximum(j - 1, 0) * BAND, BAND)
            n_in_class = lax.bitwise_and(j, blocks_per_class - 1)
            variant = jnp.where(n_in_class == 0, jnp.where(j > 0, 1, 2), 0)
            q = q_ref[0, 0, pl.ds(r0, BAND), :] * (HEAD_DIM ** -0.5)
            res_class = lax.shift_right_logical(j, class_shift)
            blocks.append(dict(
                q=q, kw=k_ref[0, 0, pl.ds(ks, 2 * BAND), :], vw=v_ref[0, 0, pl.ds(ks, 2 * BAND), :], variant=variant,
                out_class=lax.bitwise_and(res_class, out_classes - 1),
                row0=lax.shift_right_logical(res_class, int(math.log2(out_classes))) + out_stride * BAND * n_in_class))
        chains = [(blk, hh) for blk in blocks for hh in range(2)]
        zero = jnp.zeros((BAND, LANES), BF16)
        s_all = [_dot_nt(jnp.where(is_a if hh == 0 else jnp.logical_not(is_a), blk["q"], zero), blk["kw"])
                 + bias_ref[0, hh, blk["variant"]] for blk, hh in chains]
        m_all = [jnp.max(s, axis=-1, keepdims=True) for s in s_all]
        e_all = [jnp.exp(s - m) for s, m in zip(s_all, m_all)]
        d_all = [jnp.sum(e, axis=-1, keepdims=True) for e in e_all]
        o_all = [_dot(e.astype(BF16), blk["vw"]) / d for e, d, (blk, _) in zip(e_all, d_all, chains)]
        lse_all = [jnp.broadcast_to(m + jnp.log(d), (BAND, LANES)) for m, d in zip(m_all, d_all)]
        for u, blk in enumerate(blocks):
            rows = pl.ds(blk["row0"], BAND, stride=out_stride)
            o_ref[0, blk["out_class"], rows, :] = jnp.where(is_a, o_all[2 * u], o_all[2 * u + 1])
            lse_ref[0, blk["out_class"], rows, :] = jnp.where(is_a, lse_all[2 * u], lse_all[2 * u + 1])
        return carry

    lax.fori_loop(0, n_blocks // ATTN_UNROLL, body, 0)


def _attn_call(qkv, bias, branch, dilation, out_classes):
    bsz, n_slabs, s, _ = qkv.shape
    hp = n_slabs // 3
    n_blocks = s // BAND
    blocks_per_class = n_blocks // dilation
    assert blocks_per_class & (blocks_per_class - 1) == 0 and n_blocks % ATTN_UNROLL == 0
    assert dilation % out_classes == 0 and out_classes & (out_classes - 1) == 0
    kern = functools.partial(_attn_kernel, dilation=dilation, out_classes=out_classes,
                             blocks_per_class=blocks_per_class, n_blocks=n_blocks)
    slab_spec = lambda t: pl.BlockSpec((1, 1, s, LANES), lambda h, b: (b, t * hp + h, 0, 0))
    out_spec = pl.BlockSpec((1, out_classes, s // out_classes, LANES), lambda h, b: (b, 0, 0, h))
    return pl.pallas_call(
        kern,
        grid=(hp, bsz),
        in_specs=[
            slab_spec(0), slab_spec(1), slab_spec(2),
            pl.BlockSpec((1, 2, 3, BAND, 2 * BAND), lambda h, b: (branch, h, 0, 0, 0)),
        ],
        out_specs=[out_spec, out_spec],
        out_shape=[jax.ShapeDtypeStruct((bsz, out_classes, s // out_classes, hp * LANES), F32)] * 2,
        compiler_params=_compiler_params(("parallel", "parallel")),
        name=f"attn_d{dilation}",
    )(qkv, qkv, qkv, bias)


def _bd_stack(x):
    xb = x.astype(BF16)
    head_of_lane = _head_of(lax.broadcasted_iota(jnp.int32, xb.shape, 1))
    zero = jnp.zeros_like(xb)
    return jnp.concatenate([jnp.where(head_of_lane == h, xb, zero) for h in range(HEADS_PER_GROUP)], axis=0)


def _delta_kernel(dl_ref, ba_ref, alog_ref, dtb_ref, ng_ref, o_ref, state_ref):
    bsz = dl_ref.shape[0]
    tc = bsz * CHUNK
    qkv_w = 3 * DELTA_WIDTH
    n_groups = DELTA_WIDTH // GROUP_W

    @pl.when(pl.program_id(0) == 0)
    def _():
        state_ref[...] = jnp.zeros_like(state_ref)

    q_raw = dl_ref[:, :, :DELTA_WIDTH].reshape(tc, DELTA_WIDTH)
    k_raw = dl_ref[:, :, DELTA_WIDTH:2 * DELTA_WIDTH].reshape(tc, DELTA_WIDTH)
    v_all = dl_ref[:, :, 2 * DELTA_WIDTH:qkv_w].reshape(tc, DELTA_WIDTH)

    r512 = _head_of(lax.broadcasted_iota(jnp.int32, (DELTA_WIDTH, DELTA_WIDTH), 0))
    c512 = _head_of(lax.broadcasted_iota(jnp.int32, (DELTA_WIDTH, DELTA_WIDTH), 1))
    head_ones = (r512 == c512).astype(BF16)

    def head_sum(x):
        return _dot(x.astype(BF16), head_ones)

    q_all = q_raw * lax.rsqrt(head_sum(q_raw * q_raw) + EPS) * (HEAD_DIM ** -0.5)
    k_all = k_raw * lax.rsqrt(head_sum(k_raw * k_raw) + EPS)

    t = ba_ref[...].reshape(tc, LANES)
    za = t + dtb_ref[...]
    softplus = jnp.maximum(za, 0.0) + jnp.log1p(jnp.exp(-jnp.abs(za)))
    g_lane = -jnp.exp(alog_ref[...]) * softplus
    er = lax.broadcasted_iota(jnp.int32, (LANES, DELTA_WIDTH), 0)
    ec = _head_of(lax.broadcasted_iota(jnp.int32, (LANES, DELTA_WIDTH), 1))
    expand_beta = (er == ec).astype(BF16)
    expand_g = (er == ec + N_DELTA_HEADS).astype(BF16)
    beta_all = _dot(_sigmoid(t).astype(BF16), expand_beta)
    g_all = jnp.zeros((tc, DELTA_WIDTH), F32)
    for part in _split3(g_lane):
        g_all = g_all + _dot(part, expand_g)

    ci = lax.broadcasted_iota(jnp.int32, (CHUNK, CHUNK), 0)
    cj = lax.broadcasted_iota(jnp.int32, (CHUNK, CHUNK), 1)
    lower_incl = (ci >= cj).astype(BF16)
    all_ones = jnp.ones((CHUNK, CHUNK), BF16)
    row_w = lax.broadcasted_iota(jnp.int32, (CHUNK, DELTA_WIDTH), 0)
    col_w = _within_head(lax.broadcasted_iota(jnp.int32, (CHUNK, DELTA_WIDTH), 1))
    upper_incl_w = row_w <= col_w
    row_g = lax.broadcasted_iota(jnp.int32, (CHUNK, GROUP_W), 0)
    col_g = _within_head(lax.broadcasted_iota(jnp.int32, (CHUNK, GROUP_W), 1))
    causal_g = row_g >= col_g
    strict_g = row_g > col_g
    eye_g = (row_g == col_g).astype(F32)
    level_masks = []
    for n in range(int(math.log2(CHUNK))):
        shr = lambda v, k: lax.shift_right_logical(v, k)
        level_masks.append((shr(row_g, n + 1) == shr(col_g, n + 1)) & (shr(row_g, n) > shr(col_g, n)))
    sr = _head_of(lax.broadcasted_iota(jnp.int32, (GROUP_W, GROUP_W), 0))
    sc = _head_of(lax.broadcasted_iota(jnp.int32, (GROUP_W, GROUP_W), 1))
    state_mask = sr == sc

    chains = []
    for c in range(bsz):
        rows = slice(c * CHUNK, (c + 1) * CHUNK)
        gb = g_all[rows]
        gcum = jnp.zeros((CHUNK, DELTA_WIDTH), F32)
        gcum_t = jnp.zeros((CHUNK, DELTA_WIDTH), F32)
        for part in _split3(gb):
            gcum = gcum + _dot(lower_incl, part)
        for part in _split3(jnp.where(upper_incl_w, gb, 0.0)):
            gcum_t = gcum_t + _dot(all_ones, part)
        g_last = gcum[CHUNK - 1:CHUNK, :]
        exp_g = jnp.exp(gcum)
        exp_dec = jnp.exp(g_last - gcum)
        exp_last = jnp.exp(g_last)

        kc, qc, vc, bb = k_all[rows], q_all[rows], v_all[rows], beta_all[rows]
        kb = kc * bb
        vb = vc * bb
        kbe = kb * exp_g
        qe = qc * exp_g
        kdec = kc * exp_dec
        for g in range(n_groups):
            gl = slice(g * GROUP_W, (g + 1) * GROUP_W)
            diff = gcum[:, gl] - gcum_t[:, gl]
            decay = jnp.where(causal_g, jnp.exp(jnp.where(causal_g, diff, 0.0)), 0.0)
            aq = _dot_nt(jnp.concatenate([kb[:, gl], qc[:, gl]], axis=0).astype(BF16), _bd_stack(kc[:, gl]))
            chains.append(dict(
                slot=c * n_groups + g, a=jnp.where(strict_g, aq[:CHUNK] * decay, 0.0),
                qk=(aq[CHUNK:] * decay).astype(BF16), qe=qe[:, gl], kdec_t=kdec[:, gl].T.astype(BF16),
                exp_last=exp_last[:, gl],
                rhs=jnp.concatenate([_bd_stack(vb[:, gl]), _bd_stack(kbe[:, gl])], axis=1)))

    t_invs = [eye_g - jnp.where(level_masks[0], ch["a"], 0.0) for ch in chains]
    for mask in level_masks[1:]:
        ys = [_dot(jnp.where(mask, ch["a"], 0.0).astype(BF16), _bd_stack(t)) for ch, t in zip(chains, t_invs)]
        t_invs = [t - _dot(t.astype(BF16), _bd_stack(y)) for t, y in zip(t_invs, ys)]
    uws = [_dot(t.astype(BF16), ch["rhs"]) for ch, t in zip(chains, t_invs)]

    states = [state_ref[ch["slot"]] for ch in chains]
    wss = [_dot(jnp.concatenate([uw[:, GROUP_W:], ch["qe"]], axis=0).astype(BF16), st.astype(BF16))
           for ch, uw, st in zip(chains, uws, states)]
    v_news = [uw[:, :GROUP_W] - ws[:CHUNK] for uw, ws in zip(uws, wss)]
    outs = [ws[CHUNK:] + _dot(ch["qk"], _bd_stack(vn)) for ch, ws, vn in zip(chains, wss, v_news)]
    upds = [_dot(ch["kdec_t"], vn.astype(BF16)) for ch, vn in zip(chains, v_news)]
    for ch, st, upd in zip(chains, states, upds):
        for quad in (slice(0, GROUP_W // 2), slice(GROUP_W // 2, GROUP_W)):
            state_ref[ch["slot"], quad, quad] = (
                st[quad, quad] * ch["exp_last"][:, quad] + jnp.where(state_mask[quad, quad], upd[quad, quad], 0.0))

    o_all = jnp.concatenate(
        [jnp.concatenate(outs[c * n_groups:(c + 1) * n_groups], axis=1) for c in range(bsz)], axis=0)
    ms = head_sum(o_all * o_all) * (1.0 / HEAD_DIM)
    z = dl_ref[:, :, qkv_w:].reshape(tc, DELTA_WIDTH)
    o_ref[...] = (o_all * lax.rsqrt(ms + EPS) * ng_ref[...] * _silu(z)).reshape(bsz, CHUNK, DELTA_WIDTH)


def _delta_call(dl, ba, alog_row, dtb_row, ng_row):
    bsz, s, wd = dl.shape
    n_groups = DELTA_WIDTH // GROUP_W
    chunk_spec = lambda w: pl.BlockSpec((bsz, CHUNK, w), lambda i: (0, i, 0))
    return pl.pallas_call(
        _delta_kernel,
        grid=(s // CHUNK,),
        in_specs=[
            chunk_spec(wd),
            chunk_spec(LANES),
            _const_spec((1, LANES)),
            _const_spec((1, LANES)),
            _const_spec((1, DELTA_WIDTH)),
        ],
        out_specs=chunk_spec(DELTA_WIDTH),
        out_shape=jax.ShapeDtypeStruct((bsz, s, DELTA_WIDTH), F32),
        scratch_shapes=[pltpu.VMEM((bsz * n_groups, GROUP_W, GROUP_W), F32)],
        compiler_params=_compiler_params(("arbitrary",)),
        name="delta",
    )(dl, ba, alog_row, dtb_row, ng_row)


def _out_ffn_kernel(x_ref, o1_ref, o2_ref, o3_ref, l1_ref, l2_ref, l3_ref, yd_ref, mod_ref,
                    wtop_ref, wbot_ref, g2_ref, wg_ref, wu_ref, wdn_ref, gf_ref, out_ref, natural_ref, *, final_norm):
    tm = x_ref.shape[1]
    groups = [slice(r, r + tm // FFN_ROW_GROUPS) for r in range(0, tm, tm // FFN_ROW_GROUPS)]

    staged = 0

    def natural_rows(ref):
        nonlocal staged
        n_classes = ref.shape[1]
        if n_classes == 1:
            return lambda rows: ref[0, 0, rows, :]
        slot = staged
        staged += 1
        n_slabs = ref.shape[3] // LANES
        for sl in range(n_slabs):
            for cls in range(n_classes):
                natural_ref[slot, sl, pl.ds(cls, tm // n_classes, stride=n_classes), :] = (
                    ref[0, cls, :, sl * LANES:(sl + 1) * LANES])
        return lambda rows: jnp.concatenate([natural_ref[slot, sl, rows, :] for sl in range(n_slabs)], axis=1)

    o_rows = [natural_rows(r) for r in (o1_ref, o2_ref, o3_ref)]
    l_rows = [natural_rows(r) for r in (l1_ref, l2_ref, l3_ref)]

    def merge(rows):
        lses = [get(rows) for get in l_rows]
        outs = [get(rows) for get in o_rows]
        mx = jnp.maximum(jnp.maximum(lses[0], lses[1]), lses[2])
        ws = [jnp.exp(l - mx) for l in lses]
        den = ws[0] + ws[1] + ws[2]
        return ((ws[0] * outs[0] + ws[1] * outs[1] + ws[2] * outs[2]) / den).astype(BF16)

    ya = [merge(rows) for rows in groups]
    y = [_dot(a, wtop_ref[...]) + _dot(yd_ref[0, rows, :].astype(BF16), wbot_ref[...]) for a, rows in zip(ya, groups)]
    x1 = [x_ref[0, rows, :] + mod_ref[0, 2:3, :] * yy for yy, rows in zip(y, groups)]
    hb = [(_rmsnorm_rows(xx, g2_ref[...]) * (1.0 + mod_ref[0, 4:5, :]) + mod_ref[0, 3:4, :]).astype(BF16) for xx in x1]
    gate = [_dot(h, wg_ref[...]) for h in hb]
    up = [_dot(h, wu_ref[...]) for h in hb]
    y2 = [_dot((_silu(g) * u).astype(BF16), wdn_ref[...]) for g, u in zip(gate, up)]
    for rows, xx, yy in zip(groups, x1, y2):
        x2 = xx + mod_ref[0, 5:6, :] * yy
        out_ref[0, rows, :] = _rmsnorm_rows(x2, gf_ref[...]) if final_norm else x2


def _out_ffn_call(x, o_list, lse_list, yd, mod3, wtop, wbot, g2, wg, wu, wdn, gf, final_norm):
    bsz, s, d = x.shape
    tm = FFN_TOKEN_TILE
    aw = o_list[0].shape[-1]
    dff = wg.shape[1]
    row_spec = lambda w: pl.BlockSpec((1, tm, w), lambda b, i: (b, i, 0))
    branch_spec = lambda t: pl.BlockSpec((1, t.shape[1], tm // t.shape[1], aw), lambda b, i: (b, 0, i, 0))
    n_staged = sum(t.shape[1] > 1 for t in o_list + lse_list)
    return pl.pallas_call(
        functools.partial(_out_ffn_kernel, final_norm=final_norm),
        grid=(bsz, s // tm),
        in_specs=[row_spec(d)] + [branch_spec(t) for t in o_list + lse_list] + [row_spec(yd.shape[-1])] + [
            pl.BlockSpec((1, 6, d), lambda b, i: (b, 0, 0)),
            _const_spec(wtop.shape), _const_spec(wbot.shape), _const_spec((1, d)),
            _const_spec((d, dff)), _const_spec((d, dff)), _const_spec((dff, d)), _const_spec((1, d)),
        ],
        out_specs=row_spec(d),
        out_shape=jax.ShapeDtypeStruct((bsz, s, d), x.dtype),
        scratch_shapes=[pltpu.VMEM((max(n_staged, 1), aw // LANES, tm, LANES), F32)],
        compiler_params=_compiler_params(("parallel", "parallel")),
        name="out_ffn",
    )(x, *o_list, *lse_list, yd, mod3, wtop, wbot, g2, wg, wu, wdn, gf)


def kernel(x, c, w_ada, b_ada, norm_attn_g, w_in, rel_bias, conv_w, a_log, dt_bias, delta_norm_g, w_out,
           norm_ffn_g, w_gate, w_up, w_down, final_norm_g):
    bsz, s, d = x.shape
    depth = w_ada.shape[0]
    assert s % PAD_UNIT == 0 and s % CHUNK == 0 and s % TOKEN_TILE == 0
    assert all(window // dilation == BAND for window, dilation in DILATED_BRANCHES)

    bias = _bias_call(_bucket_index_tables(), rel_bias)
    lane_pad = LANES - 2 * N_DELTA_HEADS
    for l in range(depth):
        mod3 = _mod_call(c, w_ada[l], b_ada[l]).reshape(bsz, 6, d)
        qkv_w = 3 * ATTN_WIDTH
        dl_w = 4 * DELTA_WIDTH
        conv_cols = 3 * DELTA_WIDTH
        wa = w_in[l][:, :qkv_w].astype(BF16)
        wc = w_in[l][:, qkv_w:qkv_w + conv_cols].astype(BF16)
        wz = w_in[l][:, qkv_w + conv_cols:qkv_w + dl_w].astype(BF16)
        wba = jnp.pad(w_in[l][:, qkv_w + dl_w:], ((0, 0), (0, lane_pad))).astype(BF16)
        *qkv_by_branch, dl, ba = _inproj_call(
            x, mod3, norm_attn_g[l].reshape(1, d), wa, wc, wz, wba, conv_w[l].astype(F32))

        o_list, lse_list = [], []
        for branch, ((_, dilation), qkv) in enumerate(zip(DILATED_BRANCHES, qkv_by_branch)):
            out_classes = max(dilation // MAX_FAST_ROW_STRIDE, 1)
            o, lse = _attn_call(qkv.reshape(bsz, qkv.shape[1], s, LANES), bias, branch, dilation, out_classes)
            o_list.append(o)
            lse_list.append(lse)

        gate_pad = lambda v: jnp.pad(v.astype(F32), (N_DELTA_HEADS, LANES - 2 * N_DELTA_HEADS)).reshape(1, LANES)
        yd = _delta_call(dl, ba, gate_pad(a_log[l]), gate_pad(dt_bias[l]),
                         jnp.tile(delta_norm_g[l].astype(F32), N_DELTA_HEADS).reshape(1, DELTA_WIDTH))

        wo = w_out[l].astype(BF16)
        x = _out_ffn_call(
            x, o_list, lse_list, yd, mod3, wo[:ATTN_WIDTH], wo[ATTN_WIDTH:], norm_ffn_g[l].reshape(1, d),
            w_gate[l].astype(BF16), w_up[l].astype(BF16), w_down[l].astype(BF16), final_norm_g.reshape(1, d),
            final_norm=(l == depth - 1))
    return x
```
